```python
import jax, jax.numpy as jnp
from jax import lax
import numpy as np

D_MODEL = 2048
BATCH = 1
SEQ = 16384
DEPTH = 1

SSD_HEADS = 32
SSD_HEAD_DIM = 64
SSD_D_INNER = SSD_HEADS * SSD_HEAD_DIM
SSD_GROUPS = 4
SSD_D_STATE = 128
SSD_CONV = 4
SSD_CHUNK = 256
SSD_CONV_DIM = SSD_D_INNER + 2 * SSD_GROUPS * SSD_D_STATE
ATTN_HEADS = 16
ATTN_HEAD_DIM = 64
ATTN_D = ATTN_HEADS * ATTN_HEAD_DIM
DILATED_PATTERNS = ((128, 1), (512, 4), (2048, 16))
ATTN_BLOCK = 128
ROPE_DIM = ATTN_HEAD_DIM // 4
ROPE_THETA = 500000.0
D_MIX = SSD_D_INNER + ATTN_D
PROJ_SIZES = (SSD_D_INNER, SSD_D_INNER, SSD_GROUPS * SSD_D_STATE, SSD_GROUPS * SSD_D_STATE,
              SSD_HEADS, ATTN_D, ATTN_D, ATTN_D)
PROJ_DIM = sum(PROJ_SIZES)
PROJ_SPLITS = tuple(int(s) for s in np.cumsum(PROJ_SIZES)[:-1])
N_EXPERT_GROUPS = 8
EXPERTS_PER_GROUP = 8
N_EXPERTS = N_EXPERT_GROUPS * EXPERTS_PER_GROUP
TOP_K_IN_GROUP = 2
D_EXPERT = 1024
MOE_BLOCK = 128
EPS = 1e-6

kernel_name = "hymba_ssd_dilated_hiermoe_layer"


def _round_up(n, m):
    return (n + m - 1) // m * m


def rmsnorm(x, w):
    xf = x.astype(jnp.float32)
    y = xf * lax.rsqrt(jnp.mean(xf * xf, axis=-1, keepdims=True) + EPS)
    return (y * w.astype(jnp.float32)).astype(x.dtype)


def causal_depthwise_conv(u, w, b):
    c = u.shape[-1]
    y = lax.conv_general_dilated(u, w[:, None, :].astype(u.dtype), window_strides=(1,),
                                 padding=[(SSD_CONV - 1, 0)],
                                 dimension_numbers=("NWC", "WIO", "NWC"),
                                 feature_group_count=c)
    return y + b.astype(u.dtype)


def ssd_chunked(xh, dt, A, Bm, Cm):
    b, s = xh.shape[:2]
    L = _round_up(s, SSD_CHUNK)
    pad = L - s
    xh = jnp.pad(xh, ((0, 0), (0, pad), (0, 0), (0, 0)))
    dt = jnp.pad(dt, ((0, 0), (0, pad), (0, 0)))
    Bm = jnp.pad(Bm, ((0, 0), (0, pad), (0, 0), (0, 0)))
    Cm = jnp.pad(Cm, ((0, 0), (0, pad), (0, 0), (0, 0)))
    c = L // SSD_CHUNK
    E = SSD_HEADS // SSD_GROUPS
    x = xh.reshape(b, c, SSD_CHUNK, SSD_GROUPS, E, SSD_HEAD_DIM)
    dtc = dt.reshape(b, c, SSD_CHUNK, SSD_GROUPS, E)
    Bc = Bm.reshape(b, c, SSD_CHUNK, SSD_GROUPS, SSD_D_STATE)
    Cc = Cm.reshape(b, c, SSD_CHUNK, SSD_GROUPS, SSD_D_STATE)
    a = (dtc * A.reshape(SSD_GROUPS, E)).transpose(0, 1, 3, 4, 2)
    cs = jnp.cumsum(a, axis=-1)
    xdt = x * dtc[..., None]
    seg = cs[..., :, None] - cs[..., None, :]
    causal = jnp.tril(jnp.ones((SSD_CHUNK, SSD_CHUNK), dtype=bool))
    Lmat = jnp.exp(jnp.where(causal, seg, -jnp.inf))
    CB = jnp.einsum("bclgn,bcsgn->bcgls", Cc, Bc)
    y_diag = jnp.einsum("bcgls,bcgels,bcsgep->bclgep", CB, Lmat, xdt)
    decay_states = jnp.exp(cs[..., -1:] - cs)
    states = jnp.einsum("bcsgn,bcges,bcsgep->bcgepn", Bc, decay_states, xdt)
    chunk_decay = jnp.exp(cs[..., -1])

    def step(hstate, inp):
        st, dec = inp
        return hstate * dec[..., None, None] + st, hstate

    h0 = jnp.zeros((b, SSD_GROUPS, E, SSD_HEAD_DIM, SSD_D_STATE), xh.dtype)
    _, prev = lax.scan(step, h0, (jnp.moveaxis(states, 1, 0), jnp.moveaxis(chunk_decay, 1, 0)))
    prev = jnp.moveaxis(prev, 0, 1)
    y_off = jnp.einsum("bclgn,bcgepn,bcgel->bclgep", Cc, prev, jnp.exp(cs))
    y = (y_diag + y_off).reshape(b, L, SSD_HEADS, SSD_HEAD_DIM)
    return y[:, :s]


def partial_rotary(t, positions):
    half = ROPE_DIM // 2
    inv_freq = 1.0 / (ROPE_THETA ** (jnp.arange(0, ROPE_DIM, 2, dtype=jnp.float32) / ROPE_DIM))
    ang = positions.astype(jnp.float32)[..., None] * inv_freq
    cos = jnp.cos(ang)[:, :, None, :]
    sin = jnp.sin(ang)[:, :, None, :]
    t1 = t[..., :half].astype(jnp.float32)
    t2 = t[..., half:ROPE_DIM].astype(jnp.float32)
    rot = jnp.concatenate([t1 * cos - t2 * sin, t2 * cos + t1 * sin], axis=-1).astype(t.dtype)
    return jnp.concatenate([rot, t[..., ROPE_DIM:]], axis=-1)


def dilated_branch(q, k, v, window, dilation):
    b, s, h, hd = q.shape
    span = window // dilation
    blk = ATTN_BLOCK
    L = _round_up(s, dilation * blk)
    n = L // dilation
    nb = n // blk

    def to_blocks(t):
        t = jnp.pad(t.astype(jnp.float32), ((0, 0), (0, L - s), (0, 0), (0, 0)))
        t = t.reshape(b, n, dilation, h, hd).transpose(0, 3, 2, 1, 4)
        return t.reshape(b, h, dilation, nb, blk, hd)

    def with_prev(t):
        prev = jnp.pad(t, ((0, 0), (0, 0), (0, 0), (1, 0), (0, 0), (0, 0)))[:, :, :, :-1]
        return jnp.concatenate([prev, t], axis=4)

    qb = to_blocks(q)
    kk = with_prev(to_blocks(k))
    vv = with_prev(to_blocks(v))
    sc = jnp.einsum("bhrnqd,bhrnkd->bhrnqk", qb, kk) * (ATTN_HEAD_DIM ** -0.5)
    qi = jnp.arange(blk)[:, None] + blk
    kj = jnp.arange(2 * blk)[None, :]
    dist = qi - kj
    band = (dist >= 0) & (dist <= span)
    key_idx = jnp.arange(nb)[:, None] * blk + jnp.arange(2 * blk)[None, :] - blk
    valid = band[None] & (key_idx >= 0)[:, None, :]
    sc = jnp.where(valid, sc, -jnp.inf)
    m = jnp.max(sc, axis=-1)
    p = jnp.exp(sc - m[..., None])
    l = jnp.sum(p, axis=-1)
    o = jnp.einsum("bhrnqk,bhrnkd->bhrnqd", p, vv)
    o = o.reshape(b, h, dilation, n, hd).transpose(0, 3, 2, 1, 4).reshape(b, L, h, hd)[:, :s]
    m = m.reshape(b, h, dilation, n).transpose(0, 3, 2, 1).reshape(b, L, h)[:, :s]
    l = l.reshape(b, h, dilation, n).transpose(0, 3, 2, 1).reshape(b, L, h)[:, :s]
    return o, m, l


def dilated_mixture_attention(q, k, v):
    outs, ms, ls = [], [], []
    for window, dilation in DILATED_PATTERNS:
        o, m, l = dilated_branch(q, k, v, window, dilation)
        outs.append(o); ms.append(m); ls.append(l)
    ms = jnp.stack(ms)
    M = jnp.max(ms, axis=0)
    wts = jnp.exp(ms - M[None])
    num = jnp.sum(wts[..., None] * jnp.stack(outs), axis=0)
    den = jnp.sum(wts * jnp.stack(ls), axis=0)
    return (num / den[..., None]).astype(q.dtype)


def hierarchical_moe(t, layer, w_group_router, b_group_router, w_expert_router, b_expert_router,
                     w_gate, w_up, w_down):
    T, d = t.shape
    g_logits = jnp.dot(t, w_group_router[layer]).astype(jnp.float32) + b_group_router[layer]
    g_prob = jax.nn.softmax(g_logits, axis=-1)
    g_val, g_idx = lax.top_k(g_prob, 1)
    e_all = jnp.einsum("td,gde->tge", t, w_expert_router[layer]).astype(jnp.float32) + b_expert_router[layer]
    e_logits = jnp.take_along_axis(e_all, g_idx[:, :, None], axis=1)[:, 0]
    e_prob = jax.nn.softmax(e_logits, axis=-1)
    e_val, e_idx = lax.top_k(e_prob, TOP_K_IN_GROUP)
    e_val = e_val / jnp.sum(e_val, axis=-1, keepdims=True)
    gates = g_val * e_val
    expert_ids = g_idx * EXPERTS_PER_GROUP + e_idx

    A = T * TOP_K_IN_GROUP
    flat_e = expert_ids.reshape(-1)
    flat_tok = jnp.repeat(jnp.arange(T, dtype=jnp.int32), TOP_K_IN_GROUP)
    flat_w = gates.reshape(-1)
    order = jnp.argsort(flat_e)
    se, stok, sw = flat_e[order], flat_tok[order], flat_w[order]
    counts = jnp.bincount(flat_e, length=N_EXPERTS)
    padded = (counts + MOE_BLOCK - 1) // MOE_BLOCK * MOE_BLOCK
    off = jnp.cumsum(counts) - counts
    pend = jnp.cumsum(padded)
    poff = pend - padded
    dest = poff[se] + jnp.arange(A) - off[se]
    R = _round_up(A + N_EXPERTS * (MOE_BLOCK - 1), MOE_BLOCK)
    NB = R // MOE_BLOCK
    row_tok = jnp.full((R,), T, dtype=jnp.int32).at[dest].set(stok)
    t_pad = jnp.concatenate([t, jnp.zeros((1, d), t.dtype)], axis=0)
    x_rows = t_pad[row_tok].reshape(NB, MOE_BLOCK, d)
    block_start = jnp.arange(NB) * MOE_BLOCK
    block_exp = jnp.minimum(jnp.searchsorted(pend, block_start, side="right"), N_EXPERTS - 1)

    def expert_block(args):
        xb, e = args
        hid = jax.nn.silu(xb @ w_gate[layer, e]) * (xb @ w_up[layer, e])
        return hid @ w_down[layer, e]

    y_rows = lax.map(expert_block, (x_rows, block_exp)).reshape(R, d)
    y_assign = y_rows[dest] * sw[:, None].astype(y_rows.dtype)
    return jax.ops.segment_sum(y_assign, stok, num_segments=T)


def setup_inputs(seed: int = 0) -> dict:
    key = jax.random.key(seed)
    ks = jax.random.split(key, 24)
    f32 = jnp.float32

    def nrm(k, shape, scale):
        return jax.random.normal(k, shape, f32) * scale

    x = nrm(ks[0], (BATCH, SEQ, D_MODEL), 1.0)
    positions = jnp.tile(jnp.arange(SEQ, dtype=jnp.int32)[None, :], (BATCH, 1))
    norm1_w = 1.0 + nrm(ks[1], (DEPTH, D_MODEL), 0.02)
    w_in = nrm(ks[2], (DEPTH, D_MODEL, PROJ_DIM), D_MODEL ** -0.5)
    conv_w = nrm(ks[3], (DEPTH, SSD_CONV, SSD_CONV_DIM), SSD_CONV ** -0.5)
    conv_b = nrm(ks[4], (DEPTH, SSD_CONV_DIM), 0.01)
    u = jax.random.uniform(ks[5], (DEPTH, SSD_HEADS), f32)
    dt0 = jnp.exp(jnp.log(1e-3) + u * (jnp.log(1e-1) - jnp.log(1e-3)))
    dt_bias = dt0 + jnp.log(-jnp.expm1(-dt0))
    A_log = jnp.log(jax.random.uniform(ks[6], (DEPTH, SSD_HEADS), f32, 1.0, 16.0))
    D_skip = 1.0 + nrm(ks[7], (DEPTH, SSD_HEADS), 0.1)
    ssd_norm_w = 1.0 + nrm(ks[8], (DEPTH, SSD_D_INNER), 0.02)
    q_norm_w = 1.0 + nrm(ks[9], (DEPTH, ATTN_HEAD_DIM), 0.02)
    k_norm_w = 1.0 + nrm(ks[10], (DEPTH, ATTN_HEAD_DIM), 0.02)
    w_out = nrm(ks[11], (DEPTH, D_MIX, D_MODEL), D_MIX ** -0.5)
    norm2_w = 1.0 + nrm(ks[12], (DEPTH, D_MODEL), 0.02)
    w_group_router = nrm(ks[13], (DEPTH, D_MODEL, N_EXPERT_GROUPS), D_MODEL ** -0.5)
    b_group_router = nrm(ks[14], (DEPTH, N_EXPERT_GROUPS), 0.01)
    w_expert_router = nrm(ks[15], (DEPTH, N_EXPERT_GROUPS, D_MODEL, EXPERTS_PER_GROUP), D_MODEL ** -0.5)
    b_expert_router = nrm(ks[16], (DEPTH, N_EXPERT_GROUPS, EXPERTS_PER_GROUP), 0.01)
    w_gate = nrm(ks[17], (DEPTH, N_EXPERTS, D_MODEL, D_EXPERT), D_MODEL ** -0.5)
    w_up = nrm(ks[18], (DEPTH, N_EXPERTS, D_MODEL, D_EXPERT), D_MODEL ** -0.5)
    w_down = nrm(ks[19], (DEPTH, N_EXPERTS, D_EXPERT, D_MODEL), D_EXPERT ** -0.5)
    return {"x": x, "positions": positions, "norm1_w": norm1_w, "w_in": w_in,
            "conv_w": conv_w, "conv_b": conv_b, "dt_bias": dt_bias, "A_log": A_log,
            "D_skip": D_skip, "ssd_norm_w": ssd_norm_w, "q_norm_w": q_norm_w,
            "k_norm_w": k_norm_w, "w_out": w_out, "norm2_w": norm2_w,
            "w_group_router": w_group_router, "b_group_router": b_group_router,
            "w_expert_router": w_expert_router, "b_expert_router": b_expert_router,
            "w_gate": w_gate, "w_up": w_up, "w_down": w_down}


def reference(x, positions, norm1_w, w_in, conv_w, conv_b, dt_bias, A_log, D_skip, ssd_norm_w,
              q_norm_w, k_norm_w, w_out, norm2_w, w_group_router, b_group_router,
              w_expert_router, b_expert_router, w_gate, w_up, w_down):
    b, s, _ = x.shape
    h = x
    for i in range(DEPTH):
        hn = rmsnorm(h, norm1_w[i])
        proj = jnp.dot(hn, w_in[i])
        z, xs, Bm, Cm, dt_raw, q, k, v = jnp.split(proj, PROJ_SPLITS, axis=-1)

        xbc = jax.nn.silu(causal_depthwise_conv(jnp.concatenate([xs, Bm, Cm], axis=-1), conv_w[i], conv_b[i]))
        xs, Bm, Cm = jnp.split(xbc, (SSD_D_INNER, SSD_D_INNER + SSD_GROUPS * SSD_D_STATE), axis=-1)
        dt = jax.nn.softplus(dt_raw.astype(jnp.float32) + dt_bias[i].astype(jnp.float32))
        A = -jnp.exp(A_log[i].astype(jnp.float32))
        xh = xs.reshape(b, s, SSD_HEADS, SSD_HEAD_DIM).astype(jnp.float32)
        y = ssd_chunked(xh, dt, A,
                        Bm.reshape(b, s, SSD_GROUPS, SSD_D_STATE).astype(jnp.float32),
                        Cm.reshape(b, s, SSD_GROUPS, SSD_D_STATE).astype(jnp.float32))
        y = y + D_skip[i].astype(jnp.float32)[:, None] * xh
        yg = (y.reshape(b, s, SSD_D_INNER) * jax.nn.silu(z.astype(jnp.float32)))
        yg = yg.reshape(b, s, SSD_GROUPS, SSD_D_INNER // SSD_GROUPS)
        yg = yg * lax.rsqrt(jnp.mean(yg * yg, axis=-1, keepdims=True) + EPS)
        y_ssd = (yg.reshape(b, s, SSD_D_INNER) * ssd_norm_w[i].astype(jnp.float32)).astype(h.dtype)

        q = rmsnorm(q.reshape(b, s, ATTN_HEADS, ATTN_HEAD_DIM), q_norm_w[i])
        k = rmsnorm(k.reshape(b, s, ATTN_HEADS, ATTN_HEAD_DIM), k_norm_w[i])
        v = v.reshape(b, s, ATTN_HEADS, ATTN_HEAD_DIM)
        q = partial_rotary(q, positions)
        k = partial_rotary(k, positions)
        y_attn = dilated_mixture_attention(q, k, v).reshape(b, s, ATTN_D).astype(h.dtype)

        mix = jnp.concatenate([y_ssd, y_attn], axis=-1)
        h = h + jnp.dot(mix, w_out[i])

        hn2 = rmsnorm(h, norm2_w[i]).reshape(b * s, D_MODEL)
        moe = hierarchical_moe(hn2, i, w_group_router, b_group_router, w_expert_router,
                               b_expert_router, w_gate, w_up, w_down)
        h = h + moe.reshape(b, s, D_MODEL).astype(h.dtype)
    return h
```

```python
import functools

import jax
import jax.numpy as jnp
import numpy as np
from jax import lax
from jax.experimental import pallas as pl
from jax.experimental.pallas import tpu as pltpu

F32 = jnp.float32
BF16 = jnp.bfloat16
I32 = jnp.int32
U32 = jnp.uint32

D_MODEL = 2048
SSD_HEADS = 32
SSD_HEAD_DIM = 64
SSD_D_INNER = 2048
SSD_GROUPS = 4
SSD_D_STATE = 128
SSD_CONV = 4
SSD_BC = SSD_GROUPS * SSD_D_STATE
SSD_CONV_DIM = SSD_D_INNER + 2 * SSD_BC
ATTN_HEADS = 16
ATTN_HEAD_DIM = 64
ATTN_D = 1024
ATTN_SPAN = 128
DILATIONS = (1, 4, 16)
ROPE_DIM = 16
ROPE_THETA = 500000.0
N_GROUPS = 8
EXPERTS_PER_GROUP = 8
N_EXPERTS = 64
D_EXPERT = 1024
EPS = 1e-6
NEG = -1e30

LANES = 128
VMEM_LIMIT = 56 * 1024 * 1024

COL_Z, COL_XS, COL_B, COL_C, COL_Q, COL_K, COL_V = 0, 2048, 4096, 4608, 5120, 6144, 7168
PROJ_COLS = 8192

SSD_L = 128
ATT_BLK = 2048
MOE_TM = 256


def _cparams(sem):
    return pltpu.CompilerParams(dimension_semantics=sem, vmem_limit_bytes=VMEM_LIMIT)


def _silu(v):
    return v * (1.0 / (1.0 + jnp.exp(-v)))


def _rope_body(pos_ref, f_ref, sg_ref, cos_ref, sin_ref):
    ang = pos_ref[...].astype(F32) * f_ref[...]
    cos_ref[...] = jnp.cos(ang)
    sin_ref[...] = jnp.sin(ang) * sg_ref[...]


def _rope_table(pos_col, f_lane, sg_lane):
    t = pos_col.shape[0]
    tt = min(t, 2048)
    return pl.pallas_call(
        _rope_body,
        grid=(t // tt,),
        in_specs=[pl.BlockSpec((tt, 1), lambda i: (i, 0)),
                  pl.BlockSpec((1, LANES), lambda i: (0, 0)),
                  pl.BlockSpec((1, LANES), lambda i: (0, 0))],
        out_specs=[pl.BlockSpec((tt, LANES), lambda i: (i, 0))] * 2,
        out_shape=[jax.ShapeDtypeStruct((t, LANES), F32)] * 2,
        compiler_params=_cparams(("arbitrary",)),
        name="rope_table",
    )(pos_col, f_lane, sg_lane)


def _inproj_body(x_ref, nw_ref, w_ref, wdt_ref, proj_ref, dt_ref, hn_ref, *, tm, rc):
    j = pl.program_id(1)

    @pl.when(j == 0)
    def _():
        def chunk(c, carry):
            r = pl.multiple_of(c * rc, rc)
            xf = x_ref[pl.ds(r, rc), :]
            ms = jnp.mean(xf * xf, axis=-1, keepdims=True)
            hn_ref[pl.ds(r, rc), :] = (xf * lax.rsqrt(ms + EPS) * nw_ref[...]).astype(BF16)
            return carry

        lax.fori_loop(0, tm // rc, chunk, 0)
        dt_ref[...] = jnp.dot(hn_ref[...], wdt_ref[...], preferred_element_type=F32)

    proj_ref[...] = jnp.dot(hn_ref[...], w_ref[...], preferred_element_type=F32).astype(BF16)


def _inproj(x2, nw, w, wdt):
    t = x2.shape[0]
    tm, tn = min(t, 1024), 512
    return pl.pallas_call(
        functools.partial(_inproj_body, tm=tm, rc=128),
        grid=(t // tm, PROJ_COLS // tn),
        in_specs=[pl.BlockSpec((tm, D_MODEL), lambda i, j: (i, 0)),
                  pl.BlockSpec((1, D_MODEL), lambda i, j: (0, 0)),
                  pl.BlockSpec((D_MODEL, tn), lambda i, j: (0, j)),
                  pl.BlockSpec((D_MODEL, LANES), lambda i, j: (0, 0))],
        out_specs=[pl.BlockSpec((tm, tn), lambda i, j: (i, j)),
                   pl.BlockSpec((tm, LANES), lambda i, j: (i, 0))],
        out_shape=[jax.ShapeDtypeStruct((t, PROJ_COLS), BF16),
                   jax.ShapeDtypeStruct((t, LANES), F32)],
        scratch_shapes=[pltpu.VMEM((tm, D_MODEL), BF16)],
        compiler_params=_cparams(("arbitrary", "arbitrary")),
        name="inproj",
    )(x2, nw, w, wdt)


def _ssd_body(z_ref, xs_ref, b_ref, c_ref, dtr_ref, cw_ref, cb_ref, dtb_ref, alog_ref, dsk_ref, nw_ref,
              hexp_ref, y_ref, ubuf, xc, bc, cc, ybuf, s_ref, *, L):
    ci = pl.program_id(0)

    @pl.when(ci == 0)
    def _():
        ubuf[0:8, :] = jnp.zeros((8, SSD_CONV_DIM), F32)
        s_ref[...] = jnp.zeros_like(s_ref)

    ubuf[8:8 + L, 0:SSD_D_INNER] = xs_ref[...].astype(F32)
    ubuf[8:8 + L, SSD_D_INNER:SSD_D_INNER + SSD_BC] = b_ref[...].astype(F32)
    ubuf[8:8 + L, SSD_D_INNER + SSD_BC:SSD_CONV_DIM] = c_ref[...].astype(F32)
    cw = 512
    for cch in range(SSD_CONV_DIM // cw):
        cs_ = slice(cw * cch, cw * cch + cw)
        acc = cb_ref[:, cs_] + cw_ref[3:4, cs_] * ubuf[8:8 + L, cs_]
        for k in range(SSD_CONV - 1):
            acc = acc + cw_ref[k:k + 1, cs_] * ubuf[5 + k:5 + k + L, cs_]
        act = _silu(acc)
        if cch < 4:
            xc[:, cs_] = act
        elif cch == 4:
            bc[...] = act
        else:
            cc[...] = act
    ubuf[0:8, :] = ubuf[L:L + 8, :]

    dt_in = dtr_ref[...] + dtb_ref[...]
    dt = jnp.maximum(dt_in, 0.0) + jnp.log1p(jnp.exp(-jnp.abs(dt_in)))
    a = dt * (-jnp.exp(alog_ref[...]))
    row = lax.broadcasted_iota(I32, (L, L), 0)
    col = lax.broadcasted_iota(I32, (L, L), 1)
    causal = col <= row
    cs = jnp.dot(causal.astype(F32), a, precision=lax.Precision.HIGHEST, preferred_element_type=F32)
    cs_last = cs[L - 1:L, :]
    wmat = jnp.exp(cs_last - cs) * dt
    cs_t = cs.T
    dt_t = dt.T
    cdec = jnp.broadcast_to(jnp.exp(cs_last), (8, LANES))
    cdec_x = jnp.dot(cdec, hexp_ref[...], precision=lax.Precision.HIGHEST,
                     preferred_element_type=F32)[0:1, :]
    lane = lax.broadcasted_iota(I32, (L, LANES), 1)
    first = lane < SSD_HEAD_DIM

    for g in range(SSD_GROUPS):
        bg = bc[:, LANES * g:LANES * g + LANES]
        cg = cc[:, LANES * g:LANES * g + LANES]
        cb = lax.dot_general(cg.astype(BF16), bg.astype(BF16), (((1,), (1,)), ((), ())),
                             preferred_element_type=F32)
        bg_t = bg.T.astype(BF16)
        xw_parts = []
        for q in range(4):
            lanes_ = slice(512 * g + LANES * q, 512 * g + LANES * q + LANES)
            x_pair = xc[:, lanes_]
            s_pair = s_ref[g, :, LANES * q:LANES * q + LANES]
            rhs = jnp.concatenate([x_pair.astype(BF16), s_pair.astype(BF16)], axis=0)
            ys, wbs = [], []
            for e2 in range(2):
                h = 8 * g + 2 * q + e2
                cs_col = jnp.broadcast_to(cs[:, h:h + 1], (L, L))
                lm = jnp.exp(jnp.where(causal, cs_col - cs_t[h:h + 1, :], NEG))
                m = cb * lm * dt_t[h:h + 1, :]
                e_col = jnp.exp(jnp.broadcast_to(cs[:, h:h + 1], (L, LANES)))
                lhs = jnp.concatenate([m.astype(BF16), (cg * e_col).astype(BF16)], axis=1)
                ys.append(jnp.dot(lhs, rhs, preferred_element_type=F32))
                wbs.append(jnp.broadcast_to(wmat[:, h:h + 1], (L, LANES)))
            ybuf[:, lanes_] = jnp.where(first, ys[0], ys[1])
            xw_parts.append((x_pair * jnp.where(first, wbs[0], wbs[1])).astype(BF16))
        xw_g = jnp.concatenate(xw_parts, axis=1)
        s_ref[g] = (s_ref[g] * cdec_x[:, 512 * g:512 * g + 512]
                    + jnp.dot(bg_t, xw_g, preferred_element_type=F32))

    for g in range(SSD_GROUPS):
        gs = slice(512 * g, 512 * g + 512)
        zf = z_ref[:, gs].astype(F32)
        yg = (ybuf[:, gs] + dsk_ref[:, gs] * xc[:, gs]) * _silu(zf)
        ms = jnp.mean(yg * yg, axis=-1, keepdims=True)
        y_ref[:, gs] = (yg * lax.rsqrt(ms + EPS) * nw_ref[:, gs]).astype(BF16)


def _ssd(proj, dtraw, cw, cb, dtb, alog, dsk, nw, hexp):
    t = proj.shape[0]
    L = SSD_L
    full = lambda shape: pl.BlockSpec(shape, lambda i: (0,) * len(shape))
    return pl.pallas_call(
        functools.partial(_ssd_body, L=L),
        grid=(t // L,),
        in_specs=[pl.BlockSpec((L, SSD_D_INNER), lambda i: (i, COL_Z // SSD_D_INNER)),
                  pl.BlockSpec((L, SSD_D_INNER), lambda i: (i, COL_XS // SSD_D_INNER)),
                  pl.BlockSpec((L, SSD_BC), lambda i: (i, COL_B // SSD_BC)),
                  pl.BlockSpec((L, SSD_BC), lambda i: (i, COL_C // SSD_BC)),
                  pl.BlockSpec((L, LANES), lambda i: (i, 0)),
                  full((SSD_CONV, SSD_CONV_DIM)), full((1, SSD_CONV_DIM)),
                  full((1, LANES)), full((1, LANES)),
                  full((1, SSD_D_INNER)), full((1, SSD_D_INNER)), full((LANES, SSD_D_INNER))],
        out_specs=pl.BlockSpec((L, SSD_D_INNER), lambda i: (i, 0)),
        out_shape=jax.ShapeDtypeStruct((t, SSD_D_INNER), BF16),
        scratch_shapes=[pltpu.VMEM((L + 8, SSD_CONV_DIM), F32),
                        pltpu.VMEM((L, SSD_D_INNER), F32),
                        pltpu.VMEM((L, SSD_BC), F32),
                        pltpu.VMEM((L, SSD_BC), F32),
                        pltpu.VMEM((L, SSD_D_INNER), F32),
                        pltpu.VMEM((SSD_GROUPS, SSD_D_STATE, 512), F32)],
        compiler_params=_cparams(("arbitrary",)),
        name="ssd_scan",
    )(proj, proj, proj, proj, dtraw, cw, cb, dtb, alog, dsk, nw, hexp)


def _attn_body(q_ref, k_ref, v_ref, cos_ref, sin_ref, qw_ref, kw_ref, hsum_ref, bias_ref, o_ref,
               q_s, k_s, v_s, acc_s, m_s, l_s, *, blk):
    j = pl.program_id(1)
    cur = (j % 2) * blk
    lane = lax.broadcasted_iota(I32, (1, LANES), 1)
    c64 = lane % ATTN_HEAD_DIM
    low = c64 < (ROPE_DIM // 2)

    def norm_rope(raw, w_lane):
        xf = raw.astype(F32)
        x2 = xf * xf
        hi = x2.astype(BF16)
        lo = (x2 - hi.astype(F32)).astype(BF16)
        ss = (jnp.dot(hi, hsum_ref[...], preferred_element_type=F32)
              + jnp.dot(lo, hsum_ref[...], preferred_element_type=F32))
        y = xf * lax.rsqrt(ss * (1.0 / ATTN_HEAD_DIM) + EPS) * w_lane
        partner = jnp.where(low, pltpu.roll(y, LANES - ROPE_DIM // 2, 1), pltpu.roll(y, ROPE_DIM // 2, 1))
        return y * cos_ref[...] + partner * sin_ref[...]

    @pl.when(j == 0)
    def _():
        k_s[pl.ds(blk, blk), :] = jnp.zeros((blk, LANES), F32)
        v_s[pl.ds(blk, blk), :] = jnp.zeros((blk, LANES), F32)

    q_s[...] = norm_rope(q_ref[...], qw_ref[...]) * (ATTN_HEAD_DIM ** -0.5)
    k_s[pl.ds(cur, blk), :] = norm_rope(k_ref[...], kw_ref[...])
    v_s[pl.ds(cur, blk), :] = v_ref[...].astype(F32)
    acc_s[...] = jnp.zeros_like(acc_s)
    l_s[...] = jnp.zeros_like(l_s)
    m_s[...] = jnp.full_like(m_s, NEG)

    lane_q = lax.broadcasted_iota(I32, (ATTN_SPAN, LANES), 1)
    first = lane_q < ATTN_HEAD_DIM
    mask0 = first.astype(F32)
    mask1 = 1.0 - mask0

    def rows(ref, start, d):
        if d == 1:
            return ref[pl.ds(start, ATTN_SPAN), :]
        return ref[pl.ds(start, ATTN_SPAN, stride=d), :]

    def tile(start, d):
        qv = rows(q_s, start, d)
        qs = jnp.concatenate([qv * mask0, qv * mask1], axis=0).astype(BF16)
        prev_start = (cur + start - ATTN_SPAN * d) & (2 * blk - 1)
        own_start = cur + start
        kt = jnp.concatenate([rows(k_s, prev_start, d), rows(k_s, own_start, d)], axis=0).astype(BF16)
        vt = jnp.concatenate([rows(v_s, prev_start, d), rows(v_s, own_start, d)], axis=0).astype(BF16)
        s = lax.dot_general(qs, kt, (((1,), (1,)), ((), ())), preferred_element_type=F32)
        no_prev = jnp.logical_and(j == 0, start < ATTN_SPAN * d)
        s = s + bias_ref[no_prev.astype(I32)]
        m_old = rows(m_s, start, d)
        l_old = rows(l_s, start, d)
        a_old = rows(acc_s, start, d)
        m_old2 = jnp.concatenate([m_old[:, 0:1], m_old[:, ATTN_HEAD_DIM:ATTN_HEAD_DIM + 1]], axis=0)
        m_new2 = jnp.maximum(m_old2, jnp.max(s, axis=-1, keepdims=True))
        p = jnp.exp(s - m_new2)
        l_t2 = jnp.sum(p, axis=-1, keepdims=True)
        alpha2 = jnp.exp(m_old2 - m_new2)
        o2 = jnp.dot(p.astype(BF16), vt, preferred_element_type=F32)

        def lanes2(v2):
            return jnp.where(first, jnp.broadcast_to(v2[:ATTN_SPAN], (ATTN_SPAN, LANES)),
                             jnp.broadcast_to(v2[ATTN_SPAN:], (ATTN_SPAN, LANES)))

        alpha = lanes2(alpha2)
        m_new = lanes2(m_new2)
        l_new = alpha * l_old + lanes2(l_t2)
        a_new = alpha * a_old + jnp.where(first, o2[:ATTN_SPAN], o2[ATTN_SPAN:])
        if d == 1:
            sl = pl.ds(start, ATTN_SPAN)
        else:
            sl = pl.ds(start, ATTN_SPAN, stride=d)
        m_s[sl, :] = m_new
        l_s[sl, :] = l_new
        acc_s[sl, :] = a_new

    for d in DILATIONS:
        n_res = d
        n_sub = blk // (d * ATTN_SPAN)

        def loop_body(i, carry, d=d, n_res=n_res):
            r = i % n_res
            sub = i // n_res
            tile(r + sub * (d * ATTN_SPAN), d)
            return carry

        lax.fori_loop(0, n_res * n_sub, loop_body, 0)

    o_ref[...] = (acc_s[...] / l_s[...]).astype(BF16)


def _attention(proj, cos_t, sin_t, qw_lane, kw_lane, hsum, bias):
    t = proj.shape[0]
    blk = ATT_BLK
    n_hp = ATTN_HEADS // 2
    colblk = lambda base: (lambda hp, j: (j, base // LANES + hp))
    full = lambda shape: pl.BlockSpec(shape, lambda hp, j: (0,) * len(shape))
    return pl.pallas_call(
        functools.partial(_attn_body, blk=blk),
        grid=(n_hp, t // blk),
        in_specs=[pl.BlockSpec((blk, LANES), colblk(COL_Q)),
                  pl.BlockSpec((blk, LANES), colblk(COL_K)),
                  pl.BlockSpec((blk, LANES), colblk(COL_V)),
                  pl.BlockSpec((blk, LANES), lambda hp, j: (j, 0)),
                  pl.BlockSpec((blk, LANES), lambda hp, j: (j, 0)),
                  full((1, LANES)), full((1, LANES)), full((LANES, LANES)),
                  full((2, 2 * ATTN_SPAN, 2 * ATTN_SPAN))],
        out_specs=pl.BlockSpec((blk, LANES), lambda hp, j: (j, hp)),
        out_shape=jax.ShapeDtypeStruct((t, ATTN_D), BF16),
        scratch_shapes=[pltpu.VMEM((blk, LANES), F32),
                        pltpu.VMEM((2 * blk, LANES), F32),
                        pltpu.VMEM((2 * blk, LANES), F32),
                        pltpu.VMEM((blk, LANES), F32),
                        pltpu.VMEM((blk, LANES), F32),
                        pltpu.VMEM((blk, LANES), F32)],
        compiler_params=_cparams(("arbitrary", "arbitrary")),
        name="dilated_attn",
    )(proj, proj, proj, cos_t, sin_t, qw_lane, kw_lane, hsum, bias)


def _pack_bf16_pair(lo_f32, hi_f32):
    lo_bits = pltpu.bitcast(lo_f32.astype(BF16).astype(F32), U32)
    hi_bits = pltpu.bitcast(hi_f32.astype(BF16).astype(F32), U32)
    return (lo_bits >> 16) | (hi_bits & jnp.uint32(0xFFFF0000))


def _unpack_bf16_pair(w):
    lo = pltpu.bitcast(w << 16, F32)
    hi = pltpu.bitcast(w & jnp.uint32(0xFFFF0000), F32)
    return lo, hi


def _outproj_body(ys_ref, ya_ref, x_ref, w_ref, nw_ref, wr_ref, h_ref, hn_ref, lg_ref):
    acc = jnp.dot(ys_ref[...], w_ref[0:SSD_D_INNER, :], preferred_element_type=F32)
    acc = acc + jnp.dot(ya_ref[...], w_ref[SSD_D_INNER:SSD_D_INNER + ATTN_D, :], preferred_element_type=F32)
    h = x_ref[...] + acc
    h_ref[...] = h
    ms = jnp.mean(h * h, axis=-1, keepdims=True)
    hn = h * lax.rsqrt(ms + EPS) * nw_ref[...]
    half = D_MODEL // 2
    hn_ref[...] = _pack_bf16_pair(hn[:, :half], hn[:, half:])
    lg_ref[...] = lax.dot_general(wr_ref[...], hn, (((1,), (1,)), ((), ())),
                                  precision=lax.Precision.HIGHEST, preferred_element_type=F32)


def _outproj(yssd, yattn, x2, w, nw, wr_t):
    t = x2.shape[0]
    tm = 512
    full = lambda shape: pl.BlockSpec(shape, lambda i: (0,) * len(shape))
    return pl.pallas_call(
        _outproj_body,
        grid=(t // tm,),
        in_specs=[pl.BlockSpec((tm, SSD_D_INNER), lambda i: (i, 0)),
                  pl.BlockSpec((tm, ATTN_D), lambda i: (i, 0)),
                  pl.BlockSpec((tm, D_MODEL), lambda i: (i, 0)),
                  full((SSD_D_INNER + ATTN_D, D_MODEL)), full((1, D_MODEL)), full((LANES, D_MODEL))],
        out_specs=[pl.BlockSpec((tm, D_MODEL), lambda i: (i, 0)),
                   pl.BlockSpec((tm, D_MODEL // 2), lambda i: (i, 0)),
                   pl.BlockSpec((LANES, tm), lambda i: (0, i))],
        out_shape=[jax.ShapeDtypeStruct((t, D_MODEL), F32),
                   jax.ShapeDtypeStruct((t, D_MODEL // 2), U32),
                   jax.ShapeDtypeStruct((LANES, t), F32)],
        compiler_params=_cparams(("arbitrary",)),
        name="outproj",
    )(yssd, yattn, x2, w, nw, wr_t)


def _route_body(lg_ref, bias_ref, u_ref, oi_ref, of_ref, cnt_ref, carry, *, tt):
    i = pl.program_id(0)

    @pl.when(i == 0)
    def _():
        carry[...] = jnp.zeros_like(carry)

    lg = lg_ref[...] + bias_ref[...]
    sub8 = lax.broadcasted_iota(I32, (8, tt), 0)
    g = lg[0:8]
    gmax = jnp.max(g, axis=0, keepdims=True)
    gidx = jnp.min(jnp.where(g == gmax, sub8, 8), axis=0, keepdims=True)
    gval = 1.0 / jnp.sum(jnp.exp(g - gmax), axis=0, keepdims=True)
    esel = jnp.zeros((8, tt), F32)
    for grp in range(N_GROUPS):
        esel = jnp.where(gidx == grp, lg[8 + 8 * grp:16 + 8 * grp], esel)
    m1 = jnp.max(esel, axis=0, keepdims=True)
    i1 = jnp.min(jnp.where(esel == m1, sub8, 8), axis=0, keepdims=True)
    em = jnp.where(sub8 == i1, -jnp.inf, esel)
    m2 = jnp.max(em, axis=0, keepdims=True)
    i2 = jnp.min(jnp.where(em == m2, sub8, 8), axis=0, keepdims=True)
    r = jnp.exp(m2 - m1)
    w1 = gval * (1.0 / (1.0 + r))
    w2 = gval * (r / (1.0 + r))
    e1 = gidx * EXPERTS_PER_GROUP + i1
    e2 = gidx * EXPERTS_PER_GROUP + i2
    sub64 = lax.broadcasted_iota(I32, (N_EXPERTS, tt), 0)
    oh1 = sub64 == e1
    oh2 = sub64 == e2
    oh = oh1.astype(F32) + oh2.astype(F32)
    pref = jnp.dot(oh.astype(BF16), u_ref[...], preferred_element_type=F32)
    excl = carry[...] + pref - 1.0
    rank1 = jnp.sum(jnp.where(oh1, excl, 0.0), axis=0, keepdims=True)
    rank2 = jnp.sum(jnp.where(oh2, excl, 0.0), axis=0, keepdims=True)
    carry[...] = carry[...] + pref[:, tt - 1:tt]
    zi = jnp.zeros((4, tt), I32)
    oi_ref[...] = jnp.concatenate([e1, e2, rank1.astype(I32), rank2.astype(I32), zi], axis=0)
    of_ref[...] = jnp.concatenate([w1, w2, jnp.zeros((6, tt), F32)], axis=0)
    cnt_ref[...] = jnp.broadcast_to(carry[...], (N_EXPERTS, LANES))


def _route(lg_t, bias_col, utri):
    t = lg_t.shape[1]
    tt = utri.shape[0]
    return pl.pallas_call(
        functools.partial(_route_body, tt=tt),
        grid=(t // tt,),
        in_specs=[pl.BlockSpec((LANES, tt), lambda i: (0, i)),
                  pl.BlockSpec((LANES, 1), lambda i: (0, 0)),
                  pl.BlockSpec((tt, tt), lambda i: (0, 0))],
        out_specs=[pl.BlockSpec((8, tt), lambda i: (0, i)),
                   pl.BlockSpec((8, tt), lambda i: (0, i)),
                   pl.BlockSpec((N_EXPERTS, LANES), lambda i: (0, 0))],
        out_shape=[jax.ShapeDtypeStruct((8, t), I32),
                   jax.ShapeDtypeStruct((8, t), F32),
                   jax.ShapeDtypeStruct((N_EXPERTS, LANES), F32)],
        scratch_shapes=[pltpu.VMEM((N_EXPERTS, 1), F32)],
        compiler_params=_cparams(("arbitrary",)),
        name="route",
    )(lg_t, bias_col, utri)


def _dest_body(oi_ref, poff_ref, d_ref, *, tt):
    sub64 = lax.broadcasted_iota(I32, (N_EXPERTS, tt), 0)
    oi = oi_ref[...]
    poff = poff_ref[...]
    d1 = jnp.sum(jnp.where(sub64 == oi[0:1], poff, 0), axis=0, keepdims=True) + oi[2:3]
    d2 = jnp.sum(jnp.where(sub64 == oi[1:2], poff, 0), axis=0, keepdims=True) + oi[3:4]
    d_ref[...] = jnp.concatenate([d1, d2, jnp.zeros((6, tt), I32)], axis=0)


def _dest(oi, poff_col):
    t = oi.shape[1]
    tt = min(t, 2048)
    return pl.pallas_call(
        functools.partial(_dest_body, tt=tt),
        grid=(t // tt,),
        in_specs=[pl.BlockSpec((8, tt), lambda i: (0, i)),
                  pl.BlockSpec((N_EXPERTS, 1), lambda i: (0, 0))],
        out_specs=pl.BlockSpec((8, tt), lambda i: (0, i)),
        out_shape=jax.ShapeDtypeStruct((8, t), I32),
        compiler_params=_cparams(("arbitrary",)),
        name="dest_rows",
    )(oi, poff_col)


def _rowtok_body(d1_ref, d2_ref, rt_ref, *, t, r_alloc):
    def clear(i, c):
        rt_ref[i] = 0
        return c

    lax.fori_loop(0, r_alloc, clear, 0, unroll=8)

    def put(i, c):
        rt_ref[d1_ref[i]] = i
        rt_ref[d2_ref[i]] = i
        return c

    lax.fori_loop(0, t, put, 0, unroll=8)


def _rowtok(d1, d2, r_alloc):
    t = d1.shape[0]
    smem = pl.BlockSpec(memory_space=pltpu.SMEM)
    return pl.pallas_call(
        functools.partial(_rowtok_body, t=t, r_alloc=r_alloc),
        in_specs=[smem, smem],
        out_specs=smem,
        out_shape=jax.ShapeDtypeStruct((r_alloc,), I32),
        name="row_tokens",
    )(d1, d2)


def _gather_body(rt_ref, nrows_ref, src_ref, dst_ref, zbuf, sem, *, gt):
    i = pl.program_id(0)
    base = i * gt

    @pl.when(base < nrows_ref[0])
    def _():
        def issue(r, c):
            tok = rt_ref[base + r]
            pltpu.make_async_copy(src_ref.at[pl.ds(tok, 1)], dst_ref.at[pl.ds(base + r, 1)], sem).start()
            return c

        lax.fori_loop(0, gt, issue, 0, unroll=8)
        pltpu.make_async_copy(src_ref.at[pl.ds(0, gt)], dst_ref.at[pl.ds(base, gt)], sem).wait()

    @pl.when(base >= nrows_ref[0])
    def _():
        zbuf[...] = jnp.zeros_like(zbuf)
        fill = pltpu.make_async_copy(zbuf, dst_ref.at[pl.ds(base, gt)], sem)
        fill.start()
        fill.wait()


def _gather_rows(row_tok, nrows, src, r_alloc):
    gt = MOE_TM
    return pl.pallas_call(
        functools.partial(_gather_body, gt=gt),
        grid_spec=pltpu.PrefetchScalarGridSpec(
            num_scalar_prefetch=2,
            grid=(r_alloc // gt,),
            in_specs=[pl.BlockSpec(memory_space=pl.ANY)],
            out_specs=pl.BlockSpec(memory_space=pl.ANY),
            scratch_shapes=[pltpu.VMEM((gt, src.shape[1]), src.dtype),
                            pltpu.SemaphoreType.DMA(())]),
        out_shape=jax.ShapeDtypeStruct((r_alloc, src.shape[1]), src.dtype),
        compiler_params=_cparams(("arbitrary",)),
        name="dispatch_gather",
    )(row_tok, nrows, src)


def _gateup_body(te_ref, tc_ref, nt_ref, x_ref, wg_ref, wu_ref, hid_ref):
    i = pl.program_id(1)

    @pl.when(i < nt_ref[0])
    def _():
        lo, hi = _unpack_bf16_pair(x_ref[...])
        lo = lo.astype(BF16)
        hi = hi.astype(BF16)
        half = D_MODEL // 2

        def mm(w_ref):
            return (jnp.dot(lo, w_ref[0:half, :].astype(BF16), preferred_element_type=F32)
                    + jnp.dot(hi, w_ref[half:D_MODEL, :].astype(BF16), preferred_element_type=F32))

        gate = mm(wg_ref)
        hid_ref[...] = (_silu(gate) * mm(wu_ref)).astype(BF16)

    @pl.when(i >= nt_ref[0])
    def _():
        hid_ref[...] = jnp.zeros_like(hid_ref)


def _gateup(te, tcl, nt, xs, w_gate, w_up, n_tiles):
    r_alloc = xs.shape[0]
    tn = 512
    return pl.pallas_call(
        _gateup_body,
        grid_spec=pltpu.PrefetchScalarGridSpec(
            num_scalar_prefetch=3,
            grid=(D_EXPERT // tn, n_tiles),
            in_specs=[pl.BlockSpec((MOE_TM, D_MODEL // 2), lambda c, i, te, tcl, nt: (tcl[i], 0)),
                      pl.BlockSpec((None, D_MODEL, tn), lambda c, i, te, tcl, nt: (te[i], 0, c)),
                      pl.BlockSpec((None, D_MODEL, tn), lambda c, i, te, tcl, nt: (te[i], 0, c))],
            out_specs=pl.BlockSpec((MOE_TM, tn), lambda c, i, te, tcl, nt: (i, c))),
        out_shape=jax.ShapeDtypeStruct((r_alloc, D_EXPERT), BF16),
        compiler_params=_cparams(("arbitrary", "arbitrary")),
        name="expert_gateup",
    )(te, tcl, nt, xs, w_gate, w_up)


def _down_body(te_ref, tc_ref, nt_ref, hid_ref, wd_ref, y_ref):
    i = pl.program_id(0)

    @pl.when(i < nt_ref[0])
    def _():
        y = jnp.dot(hid_ref[...], wd_ref[...].astype(BF16), preferred_element_type=F32)
        half = D_MODEL // 2
        y_ref[...] = _pack_bf16_pair(y[:, :half], y[:, half:])

    @pl.when(i >= nt_ref[0])
    def _():
        y_ref[...] = jnp.zeros_like(y_ref)


def _down(te, tcl, nt, hid, w_down, n_tiles):
    r_alloc = hid.shape[0]
    return pl.pallas_call(
        _down_body,
        grid_spec=pltpu.PrefetchScalarGridSpec(
            num_scalar_prefetch=3,
            grid=(n_tiles,),
            in_specs=[pl.BlockSpec((MOE_TM, D_EXPERT), lambda i, te, tcl, nt: (tcl[i], 0)),
                      pl.BlockSpec((None, D_EXPERT, D_MODEL), lambda i, te, tcl, nt: (te[i], 0, 0))],
            out_specs=pl.BlockSpec((MOE_TM, D_MODEL // 2), lambda i, te, tcl, nt: (i, 0))),
        out_shape=jax.ShapeDtypeStruct((r_alloc, D_MODEL // 2), U32),
        compiler_params=_cparams(("arbitrary",)),
        name="expert_down",
    )(te, tcl, nt, hid, w_down)


def _combine_body(d1_ref, d2_ref, h_ref, w_ref, y_ref, o_ref, buf, sem, *, tt, n_steps):
    i = pl.program_id(0)

    def copies(step, slot, r):
        t0 = step * tt + r
        c1 = pltpu.make_async_copy(y_ref.at[pl.ds(d1_ref[t0], 1)], buf.at[slot, 0, pl.ds(r, 1)], sem.at[slot])
        c2 = pltpu.make_async_copy(y_ref.at[pl.ds(d2_ref[t0], 1)], buf.at[slot, 1, pl.ds(r, 1)], sem.at[slot])
        return c1, c2

    def issue(step, slot):
        def body(r, c):
            c1, c2 = copies(step, slot, r)
            c1.start()
            c2.start()
            return c

        lax.fori_loop(0, tt, body, 0, unroll=8)

    @pl.when(i == 0)
    def _():
        issue(0, 0)

    @pl.when(i + 1 < n_steps)
    def _():
        issue(i + 1, (i + 1) % 2)

    slot = i % 2
    pltpu.make_async_copy(y_ref.at[pl.ds(0, tt)], buf.at[slot, 0], sem.at[slot]).wait()
    pltpu.make_async_copy(y_ref.at[pl.ds(0, tt)], buf.at[slot, 1], sem.at[slot]).wait()
    a_lo, a_hi = _unpack_bf16_pair(buf[slot, 0])
    b_lo, b_hi = _unpack_bf16_pair(buf[slot, 1])
    w1 = w_ref[:, 0:1]
    w2 = w_ref[:, 1:2]
    half = D_MODEL // 2
    o_ref[:, 0:half] = h_ref[:, 0:half] + (a_lo * w1 + b_lo * w2)
    o_ref[:, half:D_MODEL] = h_ref[:, half:D_MODEL] + (a_hi * w1 + b_hi * w2)


def _combine(d1, d2, h1, w_tok, y_rows):
    t = h1.shape[0]
    tt = 256
    n_steps = t // tt
    return pl.pallas_call(
        functools.partial(_combine_body, tt=tt, n_steps=n_steps),
        grid_spec=pltpu.PrefetchScalarGridSpec(
            num_scalar_prefetch=2,
            grid=(n_steps,),
            in_specs=[pl.BlockSpec((tt, D_MODEL), lambda i, d1, d2: (i, 0)),
                      pl.BlockSpec((tt, LANES), lambda i, d1, d2: (i, 0)),
                      pl.BlockSpec(memory_space=pl.ANY)],
            out_specs=pl.BlockSpec((tt, D_MODEL), lambda i, d1, d2: (i, 0)),
            scratch_shapes=[pltpu.VMEM((2, 2, tt, D_MODEL // 2), U32),
                            pltpu.SemaphoreType.DMA((2,))]),
        out_shape=jax.ShapeDtypeStruct((t, D_MODEL), F32),
        compiler_params=_cparams(("arbitrary",)),
        name="moe_combine",
    )(d1, d2, h1, w_tok, y_rows)


def _band_bias():
    qi = np.arange(ATTN_SPAN)[:, None]
    kj = np.arange(ATTN_SPAN)[None, :]
    prev = np.where(kj >= qi, 0.0, NEG)
    own = np.where(kj <= qi, 0.0, NEG)
    with_prev = np.concatenate([prev, own], axis=1)
    no_prev = np.concatenate([np.full_like(prev, NEG), own], axis=1)
    both = np.stack([np.tile(with_prev, (2, 1)), np.tile(no_prev, (2, 1))])
    return jnp.asarray(both, F32)


def kernel(x, positions, norm1_w, w_in, conv_w, conv_b, dt_bias, A_log, D_skip, ssd_norm_w, q_norm_w, k_norm_w,
           w_out, norm2_w, w_group_router, b_group_router, w_expert_router, b_expert_router, w_gate, w_up, w_down):
    b, s, _ = x.shape
    assert b == 1 and norm1_w.shape[0] == 1
    t = s
    x2 = x.reshape(t, D_MODEL)

    w_in0 = w_in[0]
    zc, xc_, bc_, cc_, dtc, qc, kc, vc = np.cumsum((0, 2048, 2048, 512, 512, 32, 1024, 1024))
    w_main = jnp.concatenate([w_in0[:, :dtc], w_in0[:, qc:]], axis=1).astype(BF16)
    w_dt = jnp.pad(w_in0[:, dtc:qc], ((0, 0), (0, LANES - SSD_HEADS))).astype(BF16)
    pad_h = lambda v: jnp.pad(v.astype(F32), (0, LANES - SSD_HEADS)).reshape(1, LANES)
    lane = np.arange(LANES)
    c64 = lane % ATTN_HEAD_DIM
    inv_freq = 1.0 / (ROPE_THETA ** (jnp.arange(0, ROPE_DIM, 2, dtype=F32) / ROPE_DIM))
    f_lane = jnp.where(jnp.asarray(c64 < ROPE_DIM), inv_freq[jnp.asarray(c64 % (ROPE_DIM // 2))], 0.0).reshape(1, LANES)
    sg_lane = jnp.asarray(np.where(c64 < ROPE_DIM // 2, -1.0, np.where(c64 < ROPE_DIM, 1.0, 0.0)), F32).reshape(1, LANES)
    hsum = jnp.asarray((lane[:, None] // ATTN_HEAD_DIM) == (lane[None, :] // ATTN_HEAD_DIM), BF16)
    hexp = jnp.asarray(np.arange(LANES)[:, None] == (np.arange(SSD_D_INNER)[None, :] // SSD_HEAD_DIM), F32)
    qw_lane = jnp.tile(q_norm_w[0].astype(F32), 2).reshape(1, LANES)
    kw_lane = jnp.tile(k_norm_w[0].astype(F32), 2).reshape(1, LANES)
    dsk = jnp.repeat(D_skip[0].astype(F32), SSD_HEAD_DIM).reshape(1, SSD_D_INNER)
    wr = jnp.concatenate([w_group_router[0],
                          jnp.transpose(w_expert_router[0], (1, 0, 2)).reshape(D_MODEL, N_EXPERTS)], axis=1)
    wr_t = jnp.pad(wr.T.astype(F32), ((0, LANES - N_GROUPS - N_EXPERTS), (0, 0)))
    br = jnp.pad(jnp.concatenate([b_group_router[0], b_expert_router[0].reshape(-1)]).astype(F32),
                 (0, LANES - N_GROUPS - N_EXPERTS)).reshape(LANES, 1)

    cos_t, sin_t = _rope_table(positions.reshape(t, 1), f_lane, sg_lane)
    proj, dtraw = _inproj(x2, norm1_w.astype(F32), w_main, w_dt)
    y_ssd = _ssd(proj, dtraw, conv_w[0].astype(F32), conv_b.astype(F32), pad_h(dt_bias[0]), pad_h(A_log[0]),
                 dsk, ssd_norm_w.astype(F32), hexp)
    y_attn = _attention(proj, cos_t, sin_t, qw_lane, kw_lane, hsum, _band_bias())
    h1, hn2p, lg_t = _outproj(y_ssd, y_attn, x2, w_out[0].astype(BF16), norm2_w.astype(F32), wr_t)

    rt_tt = 512
    utri = jnp.asarray(np.arange(rt_tt)[:, None] <= np.arange(rt_tt)[None, :], BF16)
    oi, of, cnt = _route(lg_t, br, utri)
    counts = cnt[:, 0].astype(I32)
    n_tiles_max = (2 * t + N_EXPERTS * (MOE_TM - 1) + MOE_TM - 1) // MOE_TM
    r_alloc = n_tiles_max * MOE_TM
    tiles_e = (counts + MOE_TM - 1) // MOE_TM
    tile_end = jnp.cumsum(tiles_e)
    n_used = tile_end[-1]
    poff = ((tile_end - tiles_e) * MOE_TM).reshape(N_EXPERTS, 1)
    tidx = jnp.minimum(jnp.arange(n_tiles_max, dtype=I32), n_used - 1)
    te = jnp.searchsorted(tile_end, tidx, side="right").astype(I32)
    nt = n_used.reshape(1).astype(I32)
    dst = _dest(oi, poff.astype(I32))
    row_tok = _rowtok(dst[0], dst[1], r_alloc)

    xs = _gather_rows(row_tok, (nt * MOE_TM).astype(I32), hn2p, r_alloc)
    hid = _gateup(te, tidx, nt, xs, w_gate[0], w_up[0], n_tiles_max)
    y_rows = _down(te, tidx, nt, hid, w_down[0], n_tiles_max)
    w_tok = jnp.pad(of[0:2].T, ((0, 0), (0, LANES - 2)))
    out = _combine(dst[0], dst[1], h1, w_tok, y_rows)
    return out.reshape(b, s, D_MODEL)
```

```python
import functools

import jax
import jax.numpy as jnp
import numpy as np
from jax import lax
from jax.experimental import pallas as pl
from jax.experimental.pallas import tpu as pltpu

F32 = jnp.float32
BF16 = jnp.bfloat16
I32 = jnp.int32
U32 = jnp.uint32

D_MODEL = 2048
SSD_HEADS = 32
SSD_HEAD_DIM = 64
SSD_D_INNER = 2048
SSD_GROUPS = 4
SSD_D_STATE = 128
SSD_CONV = 4
SSD_BC = SSD_GROUPS * SSD_D_STATE
SSD_CONV_DIM = SSD_D_INNER + 2 * SSD_BC
ATTN_HEADS = 16
ATTN_HEAD_DIM = 64
ATTN_D = 1024
ATTN_SPAN = 128
DILATIONS = (1, 4, 16)
ROPE_DIM = 16
ROPE_THETA = 500000.0
N_GROUPS = 8
EXPERTS_PER_GROUP = 8
N_EXPERTS = 64
D_EXPERT = 1024
EPS = 1e-6
NEG = -1e30

LANES = 128
VMEM_LIMIT = 56 * 1024 * 1024

COL_Z, COL_XS, COL_B, COL_C, COL_Q, COL_K, COL_V = 0, 2048, 4096, 4608, 5120, 6144, 7168
PROJ_COLS = 8192

SSD_L = 128
ATT_BLK = 2048
MOE_TM = 256


def _cparams(sem):
    return pltpu.CompilerParams(dimension_semantics=sem, vmem_limit_bytes=VMEM_LIMIT)


def _silu(v):
    return v * (1.0 / (1.0 + jnp.exp(-v)))


def _rope_body(pos_ref, f_ref, sg_ref, cos_ref, sin_ref):
    ang = pos_ref[...].astype(F32) * f_ref[...]
    cos_ref[...] = jnp.cos(ang)
    sin_ref[...] = jnp.sin(ang) * sg_ref[...]


def _rope_table(pos_col, f_lane, sg_lane):
    t = pos_col.shape[0]
    tt = min(t, 2048)
    return pl.pallas_call(
        _rope_body,
        grid=(t // tt,),
        in_specs=[pl.BlockSpec((tt, 1), lambda i: (i, 0)),
                  pl.BlockSpec((1, LANES), lambda i: (0, 0)),
                  pl.BlockSpec((1, LANES), lambda i: (0, 0))],
        out_specs=[pl.BlockSpec((tt, LANES), lambda i: (i, 0))] * 2,
        out_shape=[jax.ShapeDtypeStruct((t, LANES), F32)] * 2,
        compiler_params=_cparams(("arbitrary",)),
        name="rope_table",
    )(pos_col, f_lane, sg_lane)


def _inproj_body(x_ref, nw_ref, w_ref, wdt_ref, proj_ref, dt_ref, hn_ref, *, tm, rc):
    j = pl.program_id(1)

    @pl.when(j == 0)
    def _():
        def chunk(c, carry):
            r = pl.multiple_of(c * rc, rc)
            xf = x_ref[pl.ds(r, rc), :]
            ms = jnp.mean(xf * xf, axis=-1, keepdims=True)
            hn_ref[pl.ds(r, rc), :] = (xf * lax.rsqrt(ms + EPS) * nw_ref[...]).astype(BF16)
            return carry

        lax.fori_loop(0, tm // rc, chunk, 0)
        dt_ref[...] = jnp.dot(hn_ref[...], wdt_ref[...], preferred_element_type=F32)

    proj_ref[...] = jnp.dot(hn_ref[...], w_ref[...], preferred_element_type=F32).astype(BF16)


def _inproj(x2, nw, w, wdt):
    t = x2.shape[0]
    tm, tn = min(t, 1024), 512
    return pl.pallas_call(
        functools.partial(_inproj_body, tm=tm, rc=128),
        grid=(t // tm, PROJ_COLS // tn),
        in_specs=[pl.BlockSpec((tm, D_MODEL), lambda i, j: (i, 0)),
                  pl.BlockSpec((1, D_MODEL), lambda i, j: (0, 0)),
                  pl.BlockSpec((D_MODEL, tn), lambda i, j: (0, j)),
                  pl.BlockSpec((D_MODEL, LANES), lambda i, j: (0, 0))],
        out_specs=[pl.BlockSpec((tm, tn), lambda i, j: (i, j)),
                   pl.BlockSpec((tm, LANES), lambda i, j: (i, 0))],
        out_shape=[jax.ShapeDtypeStruct((t, PROJ_COLS), BF16),
                   jax.ShapeDtypeStruct((t, LANES), F32)],
        scratch_shapes=[pltpu.VMEM((tm, D_MODEL), BF16)],
        compiler_params=_cparams(("arbitrary", "arbitrary")),
        name="inproj",
    )(x2, nw, w, wdt)


def _ssd_body(z_ref, xs_ref, b_ref, c_ref, dtr_ref, cw_ref, cb_ref, dtb_ref, alog_ref, dsk_ref, nw_ref,
              hexp_ref, y_ref, ubuf, xc, bc, cc, ybuf, s_ref, *, L):
    ci = pl.program_id(0)

    @pl.when(ci == 0)
    def _():
        ubuf[0:8, :] = jnp.zeros((8, SSD_CONV_DIM), F32)
        s_ref[...] = jnp.zeros_like(s_ref)

    ubuf[8:8 + L, 0:SSD_D_INNER] = xs_ref[...].astype(F32)
    ubuf[8:8 + L, SSD_D_INNER:SSD_D_INNER + SSD_BC] = b_ref[...].astype(F32)
    ubuf[8:8 + L, SSD_D_INNER + SSD_BC:SSD_CONV_DIM] = c_ref[...].astype(F32)
    cw = 512
    for cch in range(SSD_CONV_DIM // cw):
        cs_ = slice(cw * cch, cw * cch + cw)
        acc = cb_ref[:, cs_] + cw_ref[3:4, cs_] * ubuf[8:8 + L, cs_]
        for k in range(SSD_CONV - 1):
            acc = acc + cw_ref[k:k + 1, cs_] * ubuf[5 + k:5 + k + L, cs_]
        act = _silu(acc)
        if cch < 4:
            xc[:, cs_] = act
        elif cch == 4:
            bc[...] = act
        else:
            cc[...] = act
    ubuf[0:8, :] = ubuf[L:L + 8, :]

    dt_in = dtr_ref[...] + dtb_ref[...]
    dt = jnp.maximum(dt_in, 0.0) + jnp.log1p(jnp.exp(-jnp.abs(dt_in)))
    a = dt * (-jnp.exp(alog_ref[...]))
    row = lax.broadcasted_iota(I32, (L, L), 0)
    col = lax.broadcasted_iota(I32, (L, L), 1)
    causal = col <= row
    cs = jnp.dot(causal.astype(F32), a, precision=lax.Precision.HIGHEST, preferred_element_type=F32)
    cs_last = cs[L - 1:L, :]
    wmat = jnp.exp(cs_last - cs) * dt
    cs_t = cs.T
    dt_t = dt.T
    cdec = jnp.broadcast_to(jnp.exp(cs_last), (8, LANES))
    cdec_x = jnp.dot(cdec, hexp_ref[...], precision=lax.Precision.HIGHEST,
                     preferred_element_type=F32)[0:1, :]
    lane = lax.broadcasted_iota(I32, (L, LANES), 1)
    first = lane < SSD_HEAD_DIM

    for g in range(SSD_GROUPS):
        bg = bc[:, LANES * g:LANES * g + LANES]
        cg = cc[:, LANES * g:LANES * g + LANES]
        cb = lax.dot_general(cg.astype(BF16), bg.astype(BF16), (((1,), (1,)), ((), ())),
                             preferred_element_type=F32)
        bg_t = bg.T.astype(BF16)
        xw_parts = []
        for q in range(4):
            lanes_ = slice(512 * g + LANES * q, 512 * g + LANES * q + LANES)
            x_pair = xc[:, lanes_]
            s_pair = s_ref[g, :, LANES * q:LANES * q + LANES]
            rhs = jnp.concatenate([x_pair.astype(BF16), s_pair.astype(BF16)], axis=0)
            ys, wbs = [], []
            for e2 in range(2):
                h = 8 * g + 2 * q + e2
                cs_col = jnp.broadcast_to(cs[:, h:h + 1], (L, L))
                lm = jnp.exp(jnp.where(causal, cs_col - cs_t[h:h + 1, :], NEG))
                m = cb * lm * dt_t[h:h + 1, :]
                e_col = jnp.exp(jnp.broadcast_to(cs[:, h:h + 1], (L, LANES)))
                lhs = jnp.concatenate([m.astype(BF16), (cg * e_col).astype(BF16)], axis=1)
                ys.append(jnp.dot(lhs, rhs, preferred_element_type=F32))
                wbs.append(jnp.broadcast_to(wmat[:, h:h + 1], (L, LANES)))
            ybuf[:, lanes_] = jnp.where(first, ys[0], ys[1])
            xw_parts.append((x_pair * jnp.where(first, wbs[0], wbs[1])).astype(BF16))
        xw_g = jnp.concatenate(xw_parts, axis=1)
        s_ref[g] = (s_ref[g] * cdec_x[:, 512 * g:512 * g + 512]
                    + jnp.dot(bg_t, xw_g, preferred_element_type=F32))

    for g in range(SSD_GROUPS):
        gs = slice(512 * g, 512 * g + 512)
        zf = z_ref[:, gs].astype(F32)
        yg = (ybuf[:, gs] + dsk_ref[:, gs] * xc[:, gs]) * _silu(zf)
        ms = jnp.mean(yg * yg, axis=-1, keepdims=True)
        y_ref[:, gs] = (yg * lax.rsqrt(ms + EPS) * nw_ref[:, gs]).astype(BF16)


def _ssd(proj, dtraw, cw, cb, dtb, alog, dsk, nw, hexp):
    t = proj.shape[0]
    L = SSD_L
    full = lambda shape: pl.BlockSpec(shape, lambda i: (0,) * len(shape))
    return pl.pallas_call(
        functools.partial(_ssd_body, L=L),
        grid=(t // L,),
        in_specs=[pl.BlockSpec((L, SSD_D_INNER), lambda i: (i, COL_Z // SSD_D_INNER)),
                  pl.BlockSpec((L, SSD_D_INNER), lambda i: (i, COL_XS // SSD_D_INNER)),
                  pl.BlockSpec((L, SSD_BC), lambda i: (i, COL_B // SSD_BC)),
                  pl.BlockSpec((L, SSD_BC), lambda i: (i, COL_C // SSD_BC)),
                  pl.BlockSpec((L, LANES), lambda i: (i, 0)),
                  full((SSD_CONV, SSD_CONV_DIM)), full((1, SSD_CONV_DIM)),
                  full((1, LANES)), full((1, LANES)),
                  full((1, SSD_D_INNER)), full((1, SSD_D_INNER)), full((LANES, SSD_D_INNER))],
        out_specs=pl.BlockSpec((L, SSD_D_INNER), lambda i: (i, 0)),
        out_shape=jax.ShapeDtypeStruct((t, SSD_D_INNER), BF16),
        scratch_shapes=[pltpu.VMEM((L + 8, SSD_CONV_DIM), F32),
                        pltpu.VMEM((L, SSD_D_INNER), F32),
                        pltpu.VMEM((L, SSD_BC), F32),
                        pltpu.VMEM((L, SSD_BC), F32),
                        pltpu.VMEM((L, SSD_D_INNER), F32),
                        pltpu.VMEM((SSD_GROUPS, SSD_D_STATE, 512), F32)],
        compiler_params=_cparams(("arbitrary",)),
        name="ssd_scan",
    )(proj, proj, proj, proj, dtraw, cw, cb, dtb, alog, dsk, nw, hexp)


def _attn_body(q_ref, k_ref, v_ref, cos_ref, sin_ref, qw_ref, kw_ref, hsum_ref, bias_ref, o_ref,
               q_s, k_s, v_s, o_b, m_b, l_b, *, blk):
    j = pl.program_id(1)
    cur = (j % 2) * blk
    lane = lax.broadcasted_iota(I32, (1, LANES), 1)
    c64 = lane % ATTN_HEAD_DIM
    low = c64 < (ROPE_DIM // 2)

    def norm_rope(raw, w_lane):
        xf = raw.astype(F32)
        x2 = xf * xf
        hi = x2.astype(BF16)
        lo = (x2 - hi.astype(F32)).astype(BF16)
        ss = (jnp.dot(hi, hsum_ref[...], preferred_element_type=F32)
              + jnp.dot(lo, hsum_ref[...], preferred_element_type=F32))
        y = xf * lax.rsqrt(ss * (1.0 / ATTN_HEAD_DIM) + EPS) * w_lane
        partner = jnp.where(low, pltpu.roll(y, LANES - ROPE_DIM // 2, 1), pltpu.roll(y, ROPE_DIM // 2, 1))
        return y * cos_ref[...] + partner * sin_ref[...]

    @pl.when(j == 0)
    def _():
        k_s[pl.ds(blk, blk), :] = jnp.zeros((blk, LANES), F32)
        v_s[pl.ds(blk, blk), :] = jnp.zeros((blk, LANES), F32)

    q_s[...] = norm_rope(q_ref[...], qw_ref[...]) * (ATTN_HEAD_DIM ** -0.5)
    k_s[pl.ds(cur, blk), :] = norm_rope(k_ref[...], kw_ref[...])
    v_s[pl.ds(cur, blk), :] = v_ref[...].astype(F32)
    lane_q = lax.broadcasted_iota(I32, (ATTN_SPAN, LANES), 1)
    first = lane_q < ATTN_HEAD_DIM
    mask0 = first.astype(F32)
    mask1 = 1.0 - mask0
    ones_v = jnp.ones((2 * ATTN_SPAN, LANES), BF16)

    def rows(ref, start, d):
        if d == 1:
            return ref[pl.ds(start, ATTN_SPAN), :]
        return ref[pl.ds(start, ATTN_SPAN, stride=d), :]

    def tile(br, start, d):
        qv = rows(q_s, start, d)
        qs = jnp.concatenate([qv * mask0, qv * mask1], axis=0).astype(BF16)
        prev_start = (cur + start - ATTN_SPAN * d) & (2 * blk - 1)
        own_start = cur + start
        kt = jnp.concatenate([rows(k_s, prev_start, d), rows(k_s, own_start, d)], axis=0).astype(BF16)
        vt = jnp.concatenate([rows(v_s, prev_start, d), rows(v_s, own_start, d)], axis=0).astype(BF16)
        s = lax.dot_general(qs, kt, (((1,), (1,)), ((), ())), preferred_element_type=F32)
        no_prev = jnp.logical_and(j == 0, start < ATTN_SPAN * d)
        s = s + bias_ref[no_prev.astype(I32)]
        m2 = jnp.max(s, axis=-1, keepdims=True)
        p = jnp.exp(s - m2).astype(BF16)
        o2 = jnp.dot(p, jnp.concatenate([vt, ones_v], axis=1), preferred_element_type=F32)
        m_rep = jnp.broadcast_to(m2, (2 * ATTN_SPAN, LANES))
        if d == 1:
            sl = pl.ds(start, ATTN_SPAN)
        else:
            sl = pl.ds(start, ATTN_SPAN, stride=d)
        o_b[br, sl, :] = jnp.where(first, o2[:ATTN_SPAN, :LANES], o2[ATTN_SPAN:, :LANES])
        l_b[br, sl, :] = jnp.where(first, o2[:ATTN_SPAN, LANES:], o2[ATTN_SPAN:, LANES:])
        m_b[br, sl, :] = jnp.where(first, m_rep[:ATTN_SPAN], m_rep[ATTN_SPAN:])

    for br, d in enumerate(DILATIONS):
        n_res = d
        n_sub = blk // (d * ATTN_SPAN)

        def loop_body(i, carry, br=br, d=d, n_res=n_res):
            r = i % n_res
            sub = i // n_res
            tile(br, r + sub * (d * ATTN_SPAN), d)
            return carry

        lax.fori_loop(0, n_res * n_sub, loop_body, 0, unroll=8)

    def merge(c, carry):
        rs = pl.ds(pl.multiple_of(c * ATTN_SPAN, ATTN_SPAN), ATTN_SPAN)
        ms = [m_b[br, rs, :] for br in range(len(DILATIONS))]
        m_all = jnp.maximum(jnp.maximum(ms[0], ms[1]), ms[2])
        ws = [jnp.exp(m - m_all) for m in ms]
        num = ws[0] * o_b[0, rs, :] + ws[1] * o_b[1, rs, :] + ws[2] * o_b[2, rs, :]
        den = ws[0] * l_b[0, rs, :] + ws[1] * l_b[1, rs, :] + ws[2] * l_b[2, rs, :]
        o_ref[rs, :] = (num / den).astype(BF16)
        return carry

    lax.fori_loop(0, blk // ATTN_SPAN, merge, 0)


def _attention(proj, cos_t, sin_t, qw_lane, kw_lane, hsum, bias):
    t = proj.shape[0]
    blk = ATT_BLK
    n_hp = ATTN_HEADS // 2
    colblk = lambda base: (lambda hp, j: (j, base // LANES + hp))
    full = lambda shape: pl.BlockSpec(shape, lambda hp, j: (0,) * len(shape))
    return pl.pallas_call(
        functools.partial(_attn_body, blk=blk),
        grid=(n_hp, t // blk),
        in_specs=[pl.BlockSpec((blk, LANES), colblk(COL_Q)),
                  pl.BlockSpec((blk, LANES), colblk(COL_K)),
                  pl.BlockSpec((blk, LANES), colblk(COL_V)),
                  pl.BlockSpec((blk, LANES), lambda hp, j: (j, 0)),
                  pl.BlockSpec((blk, LANES), lambda hp, j: (j, 0)),
                  full((1, LANES)), full((1, LANES)), full((LANES, LANES)),
                  full((2, 2 * ATTN_SPAN, 2 * ATTN_SPAN))],
        out_specs=pl.BlockSpec((blk, LANES), lambda hp, j: (j, hp)),
        out_shape=jax.ShapeDtypeStruct((t, ATTN_D), BF16),
        scratch_shapes=[pltpu.VMEM((blk, LANES), F32),
                        pltpu.VMEM((2 * blk, LANES), F32),
                        pltpu.VMEM((2 * blk, LANES), F32),
                        pltpu.VMEM((len(DILATIONS), blk, LANES), F32),
                        pltpu.VMEM((len(DILATIONS), blk, LANES), F32),
                        pltpu.VMEM((len(DILATIONS), blk, LANES), F32)],
        compiler_params=_cparams(("arbitrary", "arbitrary")),
        name="dilated_attn",
    )(proj, proj, proj, cos_t, sin_t, qw_lane, kw_lane, hsum, bias)


def _pack_bf16_pair(lo_f32, hi_f32):
    lo_bits = pltpu.bitcast(lo_f32.astype(BF16).astype(F32), U32)
    hi_bits = pltpu.bitcast(hi_f32.astype(BF16).astype(F32), U32)
    return (lo_bits >> 16) | (hi_bits & jnp.uint32(0xFFFF0000))


def _unpack_bf16_pair(w):
    lo = pltpu.bitcast(w << 16, F32)
    hi = pltpu.bitcast(w & jnp.uint32(0xFFFF0000), F32)
    return lo, hi


def _outproj_body(ys_ref, ya_ref, x_ref, w_ref, nw_ref, wr_ref, h_ref, hn_ref, lg_ref):
    acc = jnp.dot(ys_ref[...], w_ref[0:SSD_D_INNER, :], preferred_element_type=F32)
    acc = acc + jnp.dot(ya_ref[...], w_ref[SSD_D_INNER:SSD_D_INNER + ATTN_D, :], preferred_element_type=F32)
    h = x_ref[...] + acc
    h_ref[...] = h
    ms = jnp.mean(h * h, axis=-1, keepdims=True)
    hn = h * lax.rsqrt(ms + EPS) * nw_ref[...]
    half = D_MODEL // 2
    hn_ref[...] = _pack_bf16_pair(hn[:, :half], hn[:, half:])
    lg_ref[...] = lax.dot_general(wr_ref[...], hn, (((1,), (1,)), ((), ())),
                                  precision=lax.Precision.HIGHEST, preferred_element_type=F32)


def _outproj(yssd, yattn, x2, w, nw, wr_t):
    t = x2.shape[0]
    tm = 512
    full = lambda shape: pl.BlockSpec(shape, lambda i: (0,) * len(shape))
    return pl.pallas_call(
        _outproj_body,
        grid=(t // tm,),
        in_specs=[pl.BlockSpec((tm, SSD_D_INNER), lambda i: (i, 0)),
                  pl.BlockSpec((tm, ATTN_D), lambda i: (i, 0)),
                  pl.BlockSpec((tm, D_MODEL), lambda i: (i, 0)),
                  full((SSD_D_INNER + ATTN_D, D_MODEL)), full((1, D_MODEL)), full((LANES, D_MODEL))],
        out_specs=[pl.BlockSpec((tm, D_MODEL), lambda i: (i, 0)),
                   pl.BlockSpec((tm, D_MODEL // 2), lambda i: (i, 0)),
                   pl.BlockSpec((LANES, tm), lambda i: (0, i))],
        out_shape=[jax.ShapeDtypeStruct((t, D_MODEL), F32),
                   jax.ShapeDtypeStruct((t, D_MODEL // 2), U32),
                   jax.ShapeDtypeStruct((LANES, t), F32)],
        compiler_params=_cparams(("arbitrary",)),
        name="outproj",
    )(yssd, yattn, x2, w, nw, wr_t)


def _route_body(lg_ref, bias_ref, u_ref, oi_ref, of_ref, cnt_ref, carry, *, tt):
    i = pl.program_id(0)

    @pl.when(i == 0)
    def _():
        carry[...] = jnp.zeros_like(carry)

    lg = lg_ref[...] + bias_ref[...]
    sub8 = lax.broadcasted_iota(I32, (8, tt), 0)
    g = lg[0:8]
    gmax = jnp.max(g, axis=0, keepdims=True)
    gidx = jnp.min(jnp.where(g == gmax, sub8, 8), axis=0, keepdims=True)
    gval = 1.0 / jnp.sum(jnp.exp(g - gmax), axis=0, keepdims=True)
    esel = jnp.zeros((8, tt), F32)
    for grp in range(N_GROUPS):
        esel = jnp.where(gidx == grp, lg[8 + 8 * grp:16 + 8 * grp], esel)
    m1 = jnp.max(esel, axis=0, keepdims=True)
    i1 = jnp.min(jnp.where(esel == m1, sub8, 8), axis=0, keepdims=True)
    em = jnp.where(sub8 == i1, -jnp.inf, esel)
    m2 = jnp.max(em, axis=0, keepdims=True)
    i2 = jnp.min(jnp.where(em == m2, sub8, 8), axis=0, keepdims=True)
    r = jnp.exp(m2 - m1)
    w1 = gval * (1.0 / (1.0 + r))
    w2 = gval * (r / (1.0 + r))
    e1 = gidx * EXPERTS_PER_GROUP + i1
    e2 = gidx * EXPERTS_PER_GROUP + i2
    sub64 = lax.broadcasted_iota(I32, (N_EXPERTS, tt), 0)
    oh1 = sub64 == e1
    oh2 = sub64 == e2
    oh = oh1.astype(F32) + oh2.astype(F32)
    pref = jnp.dot(oh.astype(BF16), u_ref[...], preferred_element_type=F32)
    excl = carry[...] + pref - 1.0
    rank1 = jnp.sum(jnp.where(oh1, excl, 0.0), axis=0, keepdims=True)
    rank2 = jnp.sum(jnp.where(oh2, excl, 0.0), axis=0, keepdims=True)
    carry[...] = carry[...] + pref[:, tt - 1:tt]
    zi = jnp.zeros((4, tt), I32)
    oi_ref[...] = jnp.concatenate([e1, e2, rank1.astype(I32), rank2.astype(I32), zi], axis=0)
    of_ref[...] = jnp.concatenate([w1, w2, jnp.zeros((6, tt), F32)], axis=0)
    cnt_ref[...] = jnp.broadcast_to(carry[...], (N_EXPERTS, LANES))


def _route(lg_t, bias_col, utri):
    t = lg_t.shape[1]
    tt = utri.shape[0]
    return pl.pallas_call(
        functools.partial(_route_body, tt=tt),
        grid=(t // tt,),
        in_specs=[pl.BlockSpec((LANES, tt), lambda i: (0, i)),
                  pl.BlockSpec((LANES, 1), lambda i: (0, 0)),
                  pl.BlockSpec((tt, tt), lambda i: (0, 0))],
        out_specs=[pl.BlockSpec((8, tt), lambda i: (0, i)),
                   pl.BlockSpec((8, tt), lambda i: (0, i)),
                   pl.BlockSpec((N_EXPERTS, LANES), lambda i: (0, 0))],
        out_shape=[jax.ShapeDtypeStruct((8, t), I32),
                   jax.ShapeDtypeStruct((8, t), F32),
                   jax.ShapeDtypeStruct((N_EXPERTS, LANES), F32)],
        scratch_shapes=[pltpu.VMEM((N_EXPERTS, 1), F32)],
        compiler_params=_cparams(("arbitrary",)),
        name="route",
    )(lg_t, bias_col, utri)


def _dest_body(oi_ref, poff_ref, d_ref, *, tt):
    sub64 = lax.broadcasted_iota(I32, (N_EXPERTS, tt), 0)
    oi = oi_ref[...]
    poff = poff_ref[...]
    d1 = jnp.sum(jnp.where(sub64 == oi[0:1], poff, 0), axis=0, keepdims=True) + oi[2:3]
    d2 = jnp.sum(jnp.where(sub64 == oi[1:2], poff, 0), axis=0, keepdims=True) + oi[3:4]
    d_ref[...] = jnp.concatenate([d1, d2, jnp.zeros((6, tt), I32)], axis=0)


def _dest(oi, poff_col):
    t = oi.shape[1]
    tt = min(t, 2048)
    return pl.pallas_call(
        functools.partial(_dest_body, tt=tt),
        grid=(t // tt,),
        in_specs=[pl.BlockSpec((8, tt), lambda i: (0, i)),
                  pl.BlockSpec((N_EXPERTS, 1), lambda i: (0, 0))],
        out_specs=pl.BlockSpec((8, tt), lambda i: (0, i)),
        out_shape=jax.ShapeDtypeStruct((8, t), I32),
        compiler_params=_cparams(("arbitrary",)),
        name="dest_rows",
    )(oi, poff_col)


def _rowtok_body(d1_ref, d2_ref, rt_ref, *, t, r_alloc):
    def clear(i, c):
        rt_ref[i] = 0
        return c

    lax.fori_loop(0, r_alloc, clear, 0, unroll=8)

    def put(i, c):
        rt_ref[d1_ref[i]] = i
        rt_ref[d2_ref[i]] = i
        return c

    lax.fori_loop(0, t, put, 0, unroll=8)


def _rowtok(d1, d2, r_alloc):
    t = d1.shape[0]
    smem = pl.BlockSpec(memory_space=pltpu.SMEM)
    return pl.pallas_call(
        functools.partial(_rowtok_body, t=t, r_alloc=r_alloc),
        in_specs=[smem, smem],
        out_specs=smem,
        out_shape=jax.ShapeDtypeStruct((r_alloc,), I32),
        name="row_tokens",
    )(d1, d2)


def _gather_body(rt_ref, nrows_ref, src_ref, o_ref, buf, sem, *, gt, n_steps):
    i = pl.program_id(0)
    nrows = nrows_ref[0]

    def issue(step, slot):
        base = step * gt

        def body(r, c):
            tok = rt_ref[base + r]
            pltpu.make_async_copy(src_ref.at[pl.ds(tok, 1)], buf.at[slot, pl.ds(r, 1)], sem.at[slot]).start()
            return c

        lax.fori_loop(0, gt, body, 0, unroll=8)

    @pl.when(i == 0)
    def _():
        issue(0, 0)

    @pl.when(jnp.logical_and(i + 1 < n_steps, (i + 1) * gt < nrows))
    def _():
        issue(i + 1, (i + 1) % 2)

    slot = i % 2

    @pl.when(jnp.logical_or(i == 0, i * gt < nrows))
    def _():
        pltpu.make_async_copy(src_ref.at[pl.ds(0, gt)], buf.at[slot], sem.at[slot]).wait()
        o_ref[...] = buf[slot]

    @pl.when(jnp.logical_and(i > 0, i * gt >= nrows))
    def _():
        o_ref[...] = jnp.zeros_like(o_ref)


def _gather_rows(row_tok, nrows, src, r_alloc):
    gt = MOE_TM
    n_steps = r_alloc // gt
    return pl.pallas_call(
        functools.partial(_gather_body, gt=gt, n_steps=n_steps),
        grid_spec=pltpu.PrefetchScalarGridSpec(
            num_scalar_prefetch=2,
            grid=(n_steps,),
            in_specs=[pl.BlockSpec(memory_space=pl.ANY)],
            out_specs=pl.BlockSpec((gt, src.shape[1]), lambda i, rt, nr: (i, 0)),
            scratch_shapes=[pltpu.VMEM((2, gt, src.shape[1]), src.dtype),
                            pltpu.SemaphoreType.DMA((2,))]),
        out_shape=jax.ShapeDtypeStruct((r_alloc, src.shape[1]), src.dtype),
        compiler_params=_cparams(("arbitrary",)),
        name="dispatch_gather",
    )(row_tok, nrows, src)


def _experts_body(ie_ref, ir_ref, int_ref, ifl_ref, ni_ref, x_hbm, wg_ref, wu_ref, wdl_ref, wdh_ref, y_hbm,
                  xlo, xhi, hid, ybuf, xbuf, sem, *, nsub):
    w = pl.program_id(0)
    s = pl.program_id(1)
    nt = int_ref[w]
    nfill = ifl_ref[w]
    row0 = ir_ref[w]
    half = D_MODEL // 2
    hn = D_EXPERT // nsub

    def tile_copy_in(tl):
        rs = pl.ds(pl.multiple_of(tl * MOE_TM, MOE_TM), MOE_TM)
        src = pl.ds(pl.multiple_of(row0 + tl * MOE_TM, MOE_TM), MOE_TM)
        return pltpu.make_async_copy(x_hbm.at[src], xbuf.at[rs], sem.at[0])

    def tile_copy_out(tl):
        rs = pl.ds(pl.multiple_of(tl * MOE_TM, MOE_TM), MOE_TM)
        dst = pl.ds(pl.multiple_of(row0 + tl * MOE_TM, MOE_TM), MOE_TM)
        return pltpu.make_async_copy(ybuf.at[rs], y_hbm.at[dst], sem.at[1])

    def for_tiles(n, fn):
        def body(tl, c):
            fn(tl)
            return c

        lax.fori_loop(0, n, body, 0)

    @pl.when(jnp.logical_and(nt > 0, s == 0))
    def _():
        for_tiles(nt, lambda tl: tile_copy_in(tl).start())
        for_tiles(nt, lambda tl: tile_copy_in(tl).wait())

        def unpack(tl):
            rs = pl.ds(pl.multiple_of(tl * MOE_TM, MOE_TM), MOE_TM)
            lo, hi = _unpack_bf16_pair(xbuf[rs, :])
            xlo[rs, :] = lo.astype(BF16)
            xhi[rs, :] = hi.astype(BF16)

        for_tiles(nt, unpack)

    for step in range(nsub):
        @pl.when(jnp.logical_and(nt > 0, s == step))
        def _(step=step):
            wg_lo = wg_ref[0:half, :].astype(BF16)
            wg_hi = wg_ref[half:D_MODEL, :].astype(BF16)
            wu_lo = wu_ref[0:half, :].astype(BF16)
            wu_hi = wu_ref[half:D_MODEL, :].astype(BF16)

            def gateup(tl):
                rs = pl.ds(pl.multiple_of(tl * MOE_TM, MOE_TM), MOE_TM)
                a = xlo[rs, :]
                b = xhi[rs, :]
                gate = (jnp.dot(a, wg_lo, preferred_element_type=F32)
                        + jnp.dot(b, wg_hi, preferred_element_type=F32))
                up = (jnp.dot(a, wu_lo, preferred_element_type=F32)
                      + jnp.dot(b, wu_hi, preferred_element_type=F32))
                hid[rs, step * hn:(step + 1) * hn] = (_silu(gate) * up).astype(BF16)

            for_tiles(nt, gateup)

    for step in range(nsub):
        @pl.when(jnp.logical_and(nt > 0, s == nsub + step))
        def _(step=step):
            wl = wdl_ref[...].astype(BF16)
            wh = wdh_ref[...].astype(BF16)

            def down(tl):
                rs = pl.ds(pl.multiple_of(tl * MOE_TM, MOE_TM), MOE_TM)
                h = hid[rs, :]
                ybuf[rs, step * hn:(step + 1) * hn] = _pack_bf16_pair(
                    jnp.dot(h, wl, preferred_element_type=F32), jnp.dot(h, wh, preferred_element_type=F32))

            for_tiles(nt, down)

    @pl.when(jnp.logical_and(nt > 0, s == 2 * nsub - 1))
    def _():
        for_tiles(nt, lambda tl: tile_copy_out(tl).start())
        for_tiles(nt, lambda tl: tile_copy_out(tl).wait())

    @pl.when(jnp.logical_and(nfill > 0, s == 0))
    def _():
        ybuf[...] = jnp.zeros_like(ybuf)
        for_tiles(nfill, lambda tl: tile_copy_out(tl).start())
        for_tiles(nfill, lambda tl: tile_copy_out(tl).wait())


ITEM_TILES = 4


def _experts(item_e, item_row0, item_nt, item_fill, n_items, xs, w_gate, w_up, w_down):
    r_alloc = xs.shape[0]
    ni = item_e.shape[0]
    nsub = 2
    hn = D_EXPERT // nsub
    rows = ITEM_TILES * MOE_TM

    def live(w, n):
        return w < n[0]

    def gu_map(w, s, ie, ir, nt, fl, n):
        return (ie[w], 0, jnp.where(live(w, n), jnp.minimum(s, nsub - 1), nsub - 1))

    def dl_map(w, s, ie, ir, nt, fl, n):
        return (ie[w], 0, jnp.where(live(w, n), jnp.clip(s - nsub, 0, nsub - 1), nsub - 1))

    def dh_map(w, s, ie, ir, nt, fl, n):
        return (ie[w], 0, nsub + jnp.where(live(w, n), jnp.clip(s - nsub, 0, nsub - 1), nsub - 1))

    return pl.pallas_call(
        functools.partial(_experts_body, nsub=nsub),
        grid_spec=pltpu.PrefetchScalarGridSpec(
            num_scalar_prefetch=5,
            grid=(ni, 2 * nsub),
            in_specs=[pl.BlockSpec(memory_space=pl.ANY),
                      pl.BlockSpec((None, D_MODEL, hn), gu_map),
                      pl.BlockSpec((None, D_MODEL, hn), gu_map),
                      pl.BlockSpec((None, D_EXPERT, hn), dl_map),
                      pl.BlockSpec((None, D_EXPERT, hn), dh_map)],
            out_specs=pl.BlockSpec(memory_space=pl.ANY),
            scratch_shapes=[pltpu.VMEM((rows, D_MODEL // 2), BF16),
                            pltpu.VMEM((rows, D_MODEL // 2), BF16),
                            pltpu.VMEM((rows, D_EXPERT), BF16),
                            pltpu.VMEM((rows, D_MODEL // 2), U32),
                            pltpu.VMEM((rows, D_MODEL // 2), U32),
                            pltpu.SemaphoreType.DMA((2,))]),
        out_shape=jax.ShapeDtypeStruct((r_alloc, D_MODEL // 2), U32),
        compiler_params=_cparams(("arbitrary", "arbitrary")),
        name="experts",
    )(item_e, item_row0, item_nt, item_fill, n_items, xs, w_gate, w_up, w_down, w_down)


def _combine_body(d1_ref, d2_ref, h_ref, w_ref, y_ref, o_ref, buf, sem, *, tt, n_steps):
    i = pl.program_id(0)

    def copies(step, slot, r):
        t0 = step * tt + r
        c1 = pltpu.make_async_copy(y_ref.at[pl.ds(d1_ref[t0], 1)], buf.at[slot, 0, pl.ds(r, 1)], sem.at[slot])
        c2 = pltpu.make_async_copy(y_ref.at[pl.ds(d2_ref[t0], 1)], buf.at[slot, 1, pl.ds(r, 1)], sem.at[slot])
        return c1, c2

    def issue(step, slot):
        def body(r, c):
            c1, c2 = copies(step, slot, r)
            c1.start()
            c2.start()
            return c

        lax.fori_loop(0, tt, body, 0, unroll=8)

    @pl.when(i == 0)
    def _():
        issue(0, 0)

    @pl.when(i + 1 < n_steps)
    def _():
        issue(i + 1, (i + 1) % 2)

    slot = i % 2
    pltpu.make_async_copy(y_ref.at[pl.ds(0, tt)], buf.at[slot, 0], sem.at[slot]).wait()
    pltpu.make_async_copy(y_ref.at[pl.ds(0, tt)], buf.at[slot, 1], sem.at[slot]).wait()
    a_lo, a_hi = _unpack_bf16_pair(buf[slot, 0])
    b_lo, b_hi = _unpack_bf16_pair(buf[slot, 1])
    w1 = w_ref[:, 0:1]
    w2 = w_ref[:, 1:2]
    half = D_MODEL // 2
    o_ref[:, 0:half] = h_ref[:, 0:half] + (a_lo * w1 + b_lo * w2)
    o_ref[:, half:D_MODEL] = h_ref[:, half:D_MODEL] + (a_hi * w1 + b_hi * w2)


def _combine(d1, d2, h1, w_tok, y_rows):
    t = h1.shape[0]
    tt = 256
    n_steps = t // tt
    return pl.pallas_call(
        functools.partial(_combine_body, tt=tt, n_steps=n_steps),
        grid_spec=pltpu.PrefetchScalarGridSpec(
            num_scalar_prefetch=2,
            grid=(n_steps,),
            in_specs=[pl.BlockSpec((tt, D_MODEL), lambda i, d1, d2: (i, 0)),
                      pl.BlockSpec((tt, LANES), lambda i, d1, d2: (i, 0)),
                      pl.BlockSpec(memory_space=pl.ANY)],
            out_specs=pl.BlockSpec((tt, D_MODEL), lambda i, d1, d2: (i, 0)),
            scratch_shapes=[pltpu.VMEM((2, 2, tt, D_MODEL // 2), U32),
                            pltpu.SemaphoreType.DMA((2,))]),
        out_shape=jax.ShapeDtypeStruct((t, D_MODEL), F32),
        compiler_params=_cparams(("arbitrary",)),
        name="moe_combine",
    )(d1, d2, h1, w_tok, y_rows)


def _band_bias():
    qi = np.arange(ATTN_SPAN)[:, None]
    kj = np.arange(ATTN_SPAN)[None, :]
    prev = np.where(kj >= qi, 0.0, NEG)
    own = np.where(kj <= qi, 0.0, NEG)
    with_prev = np.concatenate([prev, own], axis=1)
    no_prev = np.concatenate([np.full_like(prev, NEG), own], axis=1)
    both = np.stack([np.tile(with_prev, (2, 1)), np.tile(no_prev, (2, 1))])
    return jnp.asarray(both, F32)


def kernel(x, positions, norm1_w, w_in, conv_w, conv_b, dt_bias, A_log, D_skip, ssd_norm_w, q_norm_w, k_norm_w,
           w_out, norm2_w, w_group_router, b_group_router, w_expert_router, b_expert_router, w_gate, w_up, w_down):
    b, s, _ = x.shape
    assert b == 1 and norm1_w.shape[0] == 1
    t = s
    x2 = x.reshape(t, D_MODEL)

    w_in0 = w_in[0]
    zc, xc_, bc_, cc_, dtc, qc, kc, vc = np.cumsum((0, 2048, 2048, 512, 512, 32, 1024, 1024))
    w_main = jnp.concatenate([w_in0[:, :dtc], w_in0[:, qc:]], axis=1).astype(BF16)
    w_dt = jnp.pad(w_in0[:, dtc:qc], ((0, 0), (0, LANES - SSD_HEADS))).astype(BF16)
    pad_h = lambda v: jnp.pad(v.astype(F32), (0, LANES - SSD_HEADS)).reshape(1, LANES)
    lane = np.arange(LANES)
    c64 = lane % ATTN_HEAD_DIM
    inv_freq = 1.0 / (ROPE_THETA ** (jnp.arange(0, ROPE_DIM, 2, dtype=F32) / ROPE_DIM))
    f_lane = jnp.where(jnp.asarray(c64 < ROPE_DIM), inv_freq[jnp.asarray(c64 % (ROPE_DIM // 2))], 0.0).reshape(1, LANES)
    sg_lane = jnp.asarray(np.where(c64 < ROPE_DIM // 2, -1.0, np.where(c64 < ROPE_DIM, 1.0, 0.0)), F32).reshape(1, LANES)
    hsum = jnp.asarray((lane[:, None] // ATTN_HEAD_DIM) == (lane[None, :] // ATTN_HEAD_DIM), BF16)
    hexp = jnp.asarray(np.arange(LANES)[:, None] == (np.arange(SSD_D_INNER)[None, :] // SSD_HEAD_DIM), F32)
    qw_lane = jnp.tile(q_norm_w[0].astype(F32), 2).reshape(1, LANES)
    kw_lane = jnp.tile(k_norm_w[0].astype(F32), 2).reshape(1, LANES)
    dsk = jnp.repeat(D_skip[0].astype(F32), SSD_HEAD_DIM).reshape(1, SSD_D_INNER)
    wr = jnp.concatenate([w_group_router[0],
                          jnp.transpose(w_expert_router[0], (1, 0, 2)).reshape(D_MODEL, N_EXPERTS)], axis=1)
    wr_t = jnp.pad(wr.T.astype(F32), ((0, LANES - N_GROUPS - N_EXPERTS), (0, 0)))
    br = jnp.pad(jnp.concatenate([b_group_router[0], b_expert_router[0].reshape(-1)]).astype(F32),
                 (0, LANES - N_GROUPS - N_EXPERTS)).reshape(LANES, 1)

    cos_t, sin_t = _rope_table(positions.reshape(t, 1), f_lane, sg_lane)
    proj, dtraw = _inproj(x2, norm1_w.astype(F32), w_main, w_dt)
    y_ssd = _ssd(proj, dtraw, conv_w[0].astype(F32), conv_b.astype(F32), pad_h(dt_bias[0]), pad_h(A_log[0]),
                 dsk, ssd_norm_w.astype(F32), hexp)
    y_attn = _attention(proj, cos_t, sin_t, qw_lane, kw_lane, hsum, _band_bias())
    h1, hn2p, lg_t = _outproj(y_ssd, y_attn, x2, w_out[0].astype(BF16), norm2_w.astype(F32), wr_t)

    rt_tt = 512
    utri = jnp.asarray(np.arange(rt_tt)[:, None] <= np.arange(rt_tt)[None, :], BF16)
    oi, of, cnt = _route(lg_t, br, utri)
    counts = cnt[:, 0].astype(I32)
    n_tiles_max = (2 * t + N_EXPERTS * (MOE_TM - 1) + MOE_TM - 1) // MOE_TM
    r_alloc = n_tiles_max * MOE_TM
    tiles_e = (counts + MOE_TM - 1) // MOE_TM
    tile_end = jnp.cumsum(tiles_e)
    n_used = tile_end[-1]
    poff = ((tile_end - tiles_e) * MOE_TM).reshape(N_EXPERTS, 1)
    nt = n_used.reshape(1).astype(I32)
    dst = _dest(oi, poff.astype(I32))
    row_tok = _rowtok(dst[0], dst[1], r_alloc)

    items_e = (tiles_e + ITEM_TILES - 1) // ITEM_TILES
    item_end = jnp.cumsum(items_e)
    n_items = item_end[-1]
    ni_max = (n_tiles_max + (ITEM_TILES - 1) * N_EXPERTS + ITEM_TILES - 1) // ITEM_TILES
    w_idx = jnp.arange(ni_max, dtype=I32)
    w_live = w_idx < n_items
    w_c = jnp.minimum(w_idx, n_items - 1)
    item_e = jnp.searchsorted(item_end, w_c, side="right").astype(I32)
    k_in_e = w_c - (item_end - items_e)[item_e]
    spare = w_idx - n_items
    item_row0 = jnp.where(w_live, poff[item_e, 0] + k_in_e * (ITEM_TILES * MOE_TM),
                          (n_used + ITEM_TILES * spare) * MOE_TM).astype(I32)
    item_nt = jnp.where(w_live, jnp.clip(tiles_e[item_e] - ITEM_TILES * k_in_e, 0, ITEM_TILES), 0).astype(I32)
    item_fill = jnp.where(w_live, 0,
                          jnp.clip(n_tiles_max - n_used - ITEM_TILES * spare, 0, ITEM_TILES)).astype(I32)

    xs = _gather_rows(row_tok, (nt * MOE_TM).astype(I32), hn2p, r_alloc)
    y_rows = _experts(item_e, item_row0, item_nt, item_fill, n_items.reshape(1).astype(I32), xs,
                      w_gate[0], w_up[0], w_down[0])
    w_tok = jnp.pad(of[0:2].T, ((0, 0), (0, LANES - 2)))
    out = _combine(dst[0], dst[1], h1, w_tok, y_rows)
    return out.reshape(b, s, D_MODEL)
```

```python
import functools

import jax
import jax.numpy as jnp
import numpy as np
from jax import lax
from jax.experimental import pallas as pl
from jax.experimental.pallas import tpu as pltpu

F32 = jnp.float32
BF16 = jnp.bfloat16
I32 = jnp.int32
U32 = jnp.uint32

D_MODEL = 2048
SSD_HEADS = 32
SSD_HEAD_DIM = 64
SSD_D_INNER = 2048
SSD_GROUPS = 4
SSD_D_STATE = 128
SSD_CONV = 4
SSD_BC = SSD_GROUPS * SSD_D_STATE
SSD_CONV_DIM = SSD_D_INNER + 2 * SSD_BC
ATTN_HEADS = 16
ATTN_HEAD_DIM = 64
ATTN_D = 1024
ATTN_SPAN = 128
DILATIONS = (1, 4, 16)
ROPE_DIM = 16
ROPE_THETA = 500000.0
N_GROUPS = 8
EXPERTS_PER_GROUP = 8
N_EXPERTS = 64
D_EXPERT = 1024
EPS = 1e-6
NEG = -1e30

LANES = 128
VMEM_LIMIT = 56 * 1024 * 1024

COL_Z, COL_XS, COL_B, COL_C, COL_Q, COL_K, COL_V = 0, 2048, 4096, 4608, 5120, 6144, 7168
PROJ_COLS = 8192

SSD_L = 128
ATT_BLK = 2048
MOE_TM = 256


def _cparams(sem):
    return pltpu.CompilerParams(dimension_semantics=sem, vmem_limit_bytes=VMEM_LIMIT)


def _silu(v):
    return v * (1.0 / (1.0 + jnp.exp(-v)))


def _rope_body(pos_ref, f_ref, sg_ref, cos_ref, sin_ref):
    ang = pos_ref[...].astype(F32) * f_ref[...]
    cos_ref[...] = jnp.cos(ang)
    sin_ref[...] = jnp.sin(ang) * sg_ref[...]


def _rope_table(pos_col, f_lane, sg_lane):
    t = pos_col.shape[0]
    tt = min(t, 2048)
    return pl.pallas_call(
        _rope_body,
        grid=(t // tt,),
        in_specs=[pl.BlockSpec((tt, 1), lambda i: (i, 0)),
                  pl.BlockSpec((1, LANES), lambda i: (0, 0)),
                  pl.BlockSpec((1, LANES), lambda i: (0, 0))],
        out_specs=[pl.BlockSpec((tt, LANES), lambda i: (i, 0))] * 2,
        out_shape=[jax.ShapeDtypeStruct((t, LANES), F32)] * 2,
        compiler_params=_cparams(("arbitrary",)),
        name="rope_table",
    )(pos_col, f_lane, sg_lane)


def _inproj_body(x_ref, nw_ref, w_ref, wdt_ref, proj_ref, dt_ref, hn_ref, *, tm, rc):
    j = pl.program_id(1)

    @pl.when(j == 0)
    def _():
        def chunk(c, carry):
            r = pl.multiple_of(c * rc, rc)
            xf = x_ref[pl.ds(r, rc), :]
            ms = jnp.mean(xf * xf, axis=-1, keepdims=True)
            hn_ref[pl.ds(r, rc), :] = (xf * lax.rsqrt(ms + EPS) * nw_ref[...]).astype(BF16)
            return carry

        lax.fori_loop(0, tm // rc, chunk, 0)
        dt_ref[...] = jnp.dot(hn_ref[...], wdt_ref[...], preferred_element_type=F32)

    proj_ref[...] = jnp.dot(hn_ref[...], w_ref[...], preferred_element_type=F32).astype(BF16)


def _inproj(x2, nw, w, wdt):
    t = x2.shape[0]
    tm, tn = min(t, 1024), 1024
    return pl.pallas_call(
        functools.partial(_inproj_body, tm=tm, rc=128),
        grid=(t // tm, PROJ_COLS // tn),
        in_specs=[pl.BlockSpec((tm, D_MODEL), lambda i, j: (i, 0)),
                  pl.BlockSpec((1, D_MODEL), lambda i, j: (0, 0)),
                  pl.BlockSpec((D_MODEL, tn), lambda i, j: (0, j)),
                  pl.BlockSpec((D_MODEL, LANES), lambda i, j: (0, 0))],
        out_specs=[pl.BlockSpec((tm, tn), lambda i, j: (i, j)),
                   pl.BlockSpec((tm, LANES), lambda i, j: (i, 0))],
        out_shape=[jax.ShapeDtypeStruct((t, PROJ_COLS), BF16),
                   jax.ShapeDtypeStruct((t, LANES), F32)],
        scratch_shapes=[pltpu.VMEM((tm, D_MODEL), BF16)],
        compiler_params=_cparams(("arbitrary", "arbitrary")),
        name="inproj",
    )(x2, nw, w, wdt)


def _ssd_body(z_ref, xs_ref, b_ref, c_ref, dtr_ref, cw_ref, cb_ref, dtb_ref, alog_ref, dsk_ref, nw_ref,
              hexp_ref, y_ref, ubuf, xc, bc, cc, ybuf, s_ref, *, L):
    ci = pl.program_id(0)

    @pl.when(ci == 0)
    def _():
        ubuf[0:8, :] = jnp.zeros((8, SSD_CONV_DIM), F32)
        s_ref[...] = jnp.zeros_like(s_ref)

    ubuf[8:8 + L, 0:SSD_D_INNER] = xs_ref[...].astype(F32)
    ubuf[8:8 + L, SSD_D_INNER:SSD_D_INNER + SSD_BC] = b_ref[...].astype(F32)
    ubuf[8:8 + L, SSD_D_INNER + SSD_BC:SSD_CONV_DIM] = c_ref[...].astype(F32)
    cw = 512
    for cch in range(SSD_CONV_DIM // cw):
        cs_ = slice(cw * cch, cw * cch + cw)
        acc = cb_ref[:, cs_] + cw_ref[3:4, cs_] * ubuf[8:8 + L, cs_]
        for k in range(SSD_CONV - 1):
            acc = acc + cw_ref[k:k + 1, cs_] * ubuf[5 + k:5 + k + L, cs_]
        act = _silu(acc)
        if cch < 4:
            xc[:, cs_] = act
        elif cch == 4:
            bc[...] = act
        else:
            cc[...] = act
    ubuf[0:8, :] = ubuf[L:L + 8, :]

    dt_in = dtr_ref[...] + dtb_ref[...]
    dt = jnp.maximum(dt_in, 0.0) + jnp.log1p(jnp.exp(-jnp.abs(dt_in)))
    a = dt * (-jnp.exp(alog_ref[...]))
    row = lax.broadcasted_iota(I32, (L, L), 0)
    col = lax.broadcasted_iota(I32, (L, L), 1)
    causal = col <= row
    cs = jnp.dot(causal.astype(F32), a, precision=lax.Precision.HIGHEST, preferred_element_type=F32)
    cs_last = cs[L - 1:L, :]
    wmat = jnp.exp(cs_last - cs) * dt
    cs_t = cs.T
    dt_t = dt.T
    cdec = jnp.broadcast_to(jnp.exp(cs_last), (8, LANES))
    cdec_x = jnp.dot(cdec, hexp_ref[...], precision=lax.Precision.HIGHEST,
                     preferred_element_type=F32)[0:1, :]
    lane = lax.broadcasted_iota(I32, (L, LANES), 1)
    first = lane < SSD_HEAD_DIM

    for g in range(SSD_GROUPS):
        bg = bc[:, LANES * g:LANES * g + LANES]
        cg = cc[:, LANES * g:LANES * g + LANES]
        cb = lax.dot_general(cg.astype(BF16), bg.astype(BF16), (((1,), (1,)), ((), ())),
                             preferred_element_type=F32)
        bg_t = bg.T.astype(BF16)
        xw_parts = []
        for q in range(4):
            lanes_ = slice(512 * g + LANES * q, 512 * g + LANES * q + LANES)
            x_pair = xc[:, lanes_]
            s_pair = s_ref[g, :, LANES * q:LANES * q + LANES]
            rhs = jnp.concatenate([x_pair.astype(BF16), s_pair.astype(BF16)], axis=0)
            ys, wbs = [], []
            for e2 in range(2):
                h = 8 * g + 2 * q + e2
                cs_col = jnp.broadcast_to(cs[:, h:h + 1], (L, L))
                lm = jnp.exp(jnp.where(causal, cs_col - cs_t[h:h + 1, :], NEG))
                m = cb * lm * dt_t[h:h + 1, :]
                e_col = jnp.exp(jnp.broadcast_to(cs[:, h:h + 1], (L, LANES)))
                lhs = jnp.concatenate([m.astype(BF16), (cg * e_col).astype(BF16)], axis=1)
                ys.append(jnp.dot(lhs, rhs, preferred_element_type=F32))
                wbs.append(jnp.broadcast_to(wmat[:, h:h + 1], (L, LANES)))
            ybuf[:, lanes_] = jnp.where(first, ys[0], ys[1])
            xw_parts.append((x_pair * jnp.where(first, wbs[0], wbs[1])).astype(BF16))
        xw_g = jnp.concatenate(xw_parts, axis=1)
        s_ref[g] = (s_ref[g] * cdec_x[:, 512 * g:512 * g + 512]
                    + jnp.dot(bg_t, xw_g, preferred_element_type=F32))

    for g in range(SSD_GROUPS):
        gs = slice(512 * g, 512 * g + 512)
        zf = z_ref[:, gs].astype(F32)
        yg = (ybuf[:, gs] + dsk_ref[:, gs] * xc[:, gs]) * _silu(zf)
        ms = jnp.mean(yg * yg, axis=-1, keepdims=True)
        y_ref[:, gs] = (yg * lax.rsqrt(ms + EPS) * nw_ref[:, gs]).astype(BF16)


def _ssd(proj, dtraw, cw, cb, dtb, alog, dsk, nw, hexp):
    t = proj.shape[0]
    L = SSD_L
    full = lambda shape: pl.BlockSpec(shape, lambda i: (0,) * len(shape))
    return pl.pallas_call(
        functools.partial(_ssd_body, L=L),
        grid=(t // L,),
        in_specs=[pl.BlockSpec((L, SSD_D_INNER), lambda i: (i, COL_Z // SSD_D_INNER)),
                  pl.BlockSpec((L, SSD_D_INNER), lambda i: (i, COL_XS // SSD_D_INNER)),
                  pl.BlockSpec((L, SSD_BC), lambda i: (i, COL_B // SSD_BC)),
                  pl.BlockSpec((L, SSD_BC), lambda i: (i, COL_C // SSD_BC)),
                  pl.BlockSpec((L, LANES), lambda i: (i, 0)),
                  full((SSD_CONV, SSD_CONV_DIM)), full((1, SSD_CONV_DIM)),
                  full((1, LANES)), full((1, LANES)),
                  full((1, SSD_D_INNER)), full((1, SSD_D_INNER)), full((LANES, SSD_D_INNER))],
        out_specs=pl.BlockSpec((L, SSD_D_INNER), lambda i: (i, 0)),
        out_shape=jax.ShapeDtypeStruct((t, SSD_D_INNER), BF16),
        scratch_shapes=[pltpu.VMEM((L + 8, SSD_CONV_DIM), F32),
                        pltpu.VMEM((L, SSD_D_INNER), F32),
                        pltpu.VMEM((L, SSD_BC), F32),
                        pltpu.VMEM((L, SSD_BC), F32),
                        pltpu.VMEM((L, SSD_D_INNER), F32),
                        pltpu.VMEM((SSD_GROUPS, SSD_D_STATE, 512), F32)],
        compiler_params=_cparams(("arbitrary",)),
        name="ssd_scan",
    )(proj, proj, proj, proj, dtraw, cw, cb, dtb, alog, dsk, nw, hexp)


def _attn_body(q_ref, k_ref, v_ref, cos_ref, sin_ref, qw_ref, kw_ref, hsum_ref, bias_ref, o_ref,
               q_s, k_s, v_s, o_b, m_b, l_b, *, blk):
    j = pl.program_id(1)
    cur = (j % 2) * blk
    lane = lax.broadcasted_iota(I32, (1, LANES), 1)
    c64 = lane % ATTN_HEAD_DIM
    low = c64 < (ROPE_DIM // 2)

    def norm_rope(raw, w_lane):
        xf = raw.astype(F32)
        x2 = xf * xf
        hi = x2.astype(BF16)
        lo = (x2 - hi.astype(F32)).astype(BF16)
        ss = (jnp.dot(hi, hsum_ref[...], preferred_element_type=F32)
              + jnp.dot(lo, hsum_ref[...], preferred_element_type=F32))
        y = xf * lax.rsqrt(ss * (1.0 / ATTN_HEAD_DIM) + EPS) * w_lane
        partner = jnp.where(low, pltpu.roll(y, LANES - ROPE_DIM // 2, 1), pltpu.roll(y, ROPE_DIM // 2, 1))
        return y * cos_ref[...] + partner * sin_ref[...]

    @pl.when(j == 0)
    def _():
        k_s[pl.ds(blk, blk), :] = jnp.zeros((blk, LANES), F32)
        v_s[pl.ds(blk, blk), :] = jnp.zeros((blk, LANES), F32)

    q_s[...] = norm_rope(q_ref[...], qw_ref[...]) * (ATTN_HEAD_DIM ** -0.5)
    k_s[pl.ds(cur, blk), :] = norm_rope(k_ref[...], kw_ref[...])
    v_s[pl.ds(cur, blk), :] = v_ref[...].astype(F32)
    lane_q = lax.broadcasted_iota(I32, (ATTN_SPAN, LANES), 1)
    first = lane_q < ATTN_HEAD_DIM
    mask0 = first.astype(F32)
    mask1 = 1.0 - mask0
    ones_v = jnp.ones((2 * ATTN_SPAN, LANES), BF16)

    def rows(ref, start, d):
        if d == 1:
            return ref[pl.ds(start, ATTN_SPAN), :]
        return ref[pl.ds(start, ATTN_SPAN, stride=d), :]

    def tile(br, start, d):
        qv = rows(q_s, start, d)
        qs = jnp.concatenate([qv * mask0, qv * mask1], axis=0).astype(BF16)
        prev_start = (cur + start - ATTN_SPAN * d) & (2 * blk - 1)
        own_start = cur + start
        kt = jnp.concatenate([rows(k_s, prev_start, d), rows(k_s, own_start, d)], axis=0).astype(BF16)
        vt = jnp.concatenate([rows(v_s, prev_start, d), rows(v_s, own_start, d)], axis=0).astype(BF16)
        s = lax.dot_general(qs, kt, (((1,), (1,)), ((), ())), preferred_element_type=F32)
        no_prev = jnp.logical_and(j == 0, start < ATTN_SPAN * d)
        s = s + bias_ref[no_prev.astype(I32)]
        m2 = jnp.max(s, axis=-1, keepdims=True)
        p = jnp.exp(s - m2).astype(BF16)
        o2 = jnp.dot(p, jnp.concatenate([vt, ones_v], axis=1), preferred_element_type=F32)
        m_rep = jnp.broadcast_to(m2, (2 * ATTN_SPAN, LANES))
        if d == 1:
            sl = pl.ds(start, ATTN_SPAN)
        else:
            sl = pl.ds(start, ATTN_SPAN, stride=d)
        o_b[br, sl, :] = jnp.where(first, o2[:ATTN_SPAN, :LANES], o2[ATTN_SPAN:, :LANES])
        l_b[br, sl, :] = jnp.where(first, o2[:ATTN_SPAN, LANES:], o2[ATTN_SPAN:, LANES:])
        m_b[br, sl, :] = jnp.where(first, m_rep[:ATTN_SPAN], m_rep[ATTN_SPAN:])

    for br, d in enumerate(DILATIONS):
        n_res = d
        n_sub = blk // (d * ATTN_SPAN)

        def loop_body(i, carry, br=br, d=d, n_res=n_res):
            r = i % n_res
            sub = i // n_res
            tile(br, r + sub * (d * ATTN_SPAN), d)
            return carry

        lax.fori_loop(0, n_res * n_sub, loop_body, 0, unroll=8)

    def merge(c, carry):
        rs = pl.ds(pl.multiple_of(c * ATTN_SPAN, ATTN_SPAN), ATTN_SPAN)
        ms = [m_b[br, rs, :] for br in range(len(DILATIONS))]
        m_all = jnp.maximum(jnp.maximum(ms[0], ms[1]), ms[2])
        ws = [jnp.exp(m - m_all) for m in ms]
        num = ws[0] * o_b[0, rs, :] + ws[1] * o_b[1, rs, :] + ws[2] * o_b[2, rs, :]
        den = ws[0] * l_b[0, rs, :] + ws[1] * l_b[1, rs, :] + ws[2] * l_b[2, rs, :]
        o_ref[rs, :] = (num / den).astype(BF16)
        return carry

    lax.fori_loop(0, blk // ATTN_SPAN, merge, 0)


def _attention(proj, cos_t, sin_t, qw_lane, kw_lane, hsum, bias):
    t = proj.shape[0]
    blk = ATT_BLK
    n_hp = ATTN_HEADS // 2
    colblk = lambda base: (lambda hp, j: (j, base // LANES + hp))
    full = lambda shape: pl.BlockSpec(shape, lambda hp, j: (0,) * len(shape))
    return pl.pallas_call(
        functools.partial(_attn_body, blk=blk),
        grid=(n_hp, t // blk),
        in_specs=[pl.BlockSpec((blk, LANES), colblk(COL_Q)),
                  pl.BlockSpec((blk, LANES), colblk(COL_K)),
                  pl.BlockSpec((blk, LANES), colblk(COL_V)),
                  pl.BlockSpec((blk, LANES), lambda hp, j: (j, 0)),
                  pl.BlockSpec((blk, LANES), lambda hp, j: (j, 0)),
                  full((1, LANES)), full((1, LANES)), full((LANES, LANES)),
                  full((2, 2 * ATTN_SPAN, 2 * ATTN_SPAN))],
        out_specs=pl.BlockSpec((blk, LANES), lambda hp, j: (j, hp)),
        out_shape=jax.ShapeDtypeStruct((t, ATTN_D), BF16),
        scratch_shapes=[pltpu.VMEM((blk, LANES), F32),
                        pltpu.VMEM((2 * blk, LANES), F32),
                        pltpu.VMEM((2 * blk, LANES), F32),
                        pltpu.VMEM((len(DILATIONS), blk, LANES), F32),
                        pltpu.VMEM((len(DILATIONS), blk, LANES), F32),
                        pltpu.VMEM((len(DILATIONS), blk, LANES), F32)],
        compiler_params=_cparams(("arbitrary", "arbitrary")),
        name="dilated_attn",
    )(proj, proj, proj, cos_t, sin_t, qw_lane, kw_lane, hsum, bias)


def _pack_bf16_pair(lo_f32, hi_f32):
    lo_bits = pltpu.bitcast(lo_f32.astype(BF16).astype(F32), U32)
    hi_bits = pltpu.bitcast(hi_f32.astype(BF16).astype(F32), U32)
    return (lo_bits >> 16) | (hi_bits & jnp.uint32(0xFFFF0000))


def _unpack_bf16_pair(w):
    lo = pltpu.bitcast(w << 16, F32)
    hi = pltpu.bitcast(w & jnp.uint32(0xFFFF0000), F32)
    return lo, hi


def _outproj_body(ys_ref, ya_ref, x_ref, w_ref, nw_ref, wr_ref, h_ref, hn_ref, lg_ref):
    acc = jnp.dot(ys_ref[...], w_ref[0:SSD_D_INNER, :], preferred_element_type=F32)
    acc = acc + jnp.dot(ya_ref[...], w_ref[SSD_D_INNER:SSD_D_INNER + ATTN_D, :], preferred_element_type=F32)
    h = x_ref[...] + acc
    h_ref[...] = h
    ms = jnp.mean(h * h, axis=-1, keepdims=True)
    hn = h * lax.rsqrt(ms + EPS) * nw_ref[...]
    half = D_MODEL // 2
    hn_ref[...] = _pack_bf16_pair(hn[:, :half], hn[:, half:])
    lg_ref[...] = lax.dot_general(wr_ref[...], hn, (((1,), (1,)), ((), ())),
                                  precision=lax.Precision.HIGHEST, preferred_element_type=F32)


def _outproj(yssd, yattn, x2, w, nw, wr_t):
    t = x2.shape[0]
    tm = 512
    full = lambda shape: pl.BlockSpec(shape, lambda i: (0,) * len(shape))
    return pl.pallas_call(
        _outproj_body,
        grid=(t // tm,),
        in_specs=[pl.BlockSpec((tm, SSD_D_INNER), lambda i: (i, 0)),
                  pl.BlockSpec((tm, ATTN_D), lambda i: (i, 0)),
                  pl.BlockSpec((tm, D_MODEL), lambda i: (i, 0)),
                  full((SSD_D_INNER + ATTN_D, D_MODEL)), full((1, D_MODEL)), full((LANES, D_MODEL))],
        out_specs=[pl.BlockSpec((tm, D_MODEL), lambda i: (i, 0)),
                   pl.BlockSpec((tm, D_MODEL // 2), lambda i: (i, 0)),
                   pl.BlockSpec((LANES, tm), lambda i: (0, i))],
        out_shape=[jax.ShapeDtypeStruct((t, D_MODEL), F32),
                   jax.ShapeDtypeStruct((t, D_MODEL // 2), U32),
                   jax.ShapeDtypeStruct((LANES, t), F32)],
        compiler_params=_cparams(("arbitrary",)),
        name="outproj",
    )(yssd, yattn, x2, w, nw, wr_t)


def _route_body(lg_ref, bias_ref, u_ref, oi_ref, of_ref, cnt_ref, carry, *, tt):
    i = pl.program_id(0)

    @pl.when(i == 0)
    def _():
        carry[...] = jnp.zeros_like(carry)

    lg = lg_ref[...] + bias_ref[...]
    sub8 = lax.broadcasted_iota(I32, (8, tt), 0)
    g = lg[0:8]
    gmax = jnp.max(g, axis=0, keepdims=True)
    gidx = jnp.min(jnp.where(g == gmax, sub8, 8), axis=0, keepdims=True)
    gval = 1.0 / jnp.sum(jnp.exp(g - gmax), axis=0, keepdims=True)
    esel = jnp.zeros((8, tt), F32)
    for grp in range(N_GROUPS):
        esel = jnp.where(gidx == grp, lg[8 + 8 * grp:16 + 8 * grp], esel)
    m1 = jnp.max(esel, axis=0, keepdims=True)
    i1 = jnp.min(jnp.where(esel == m1, sub8, 8), axis=0, keepdims=True)
    em = jnp.where(sub8 == i1, -jnp.inf, esel)
    m2 = jnp.max(em, axis=0, keepdims=True)
    i2 = jnp.min(jnp.where(em == m2, sub8, 8), axis=0, keepdims=True)
    r = jnp.exp(m2 - m1)
    w1 = gval * (1.0 / (1.0 + r))
    w2 = gval * (r / (1.0 + r))
    e1 = gidx * EXPERTS_PER_GROUP + i1
    e2 = gidx * EXPERTS_PER_GROUP + i2
    sub64 = lax.broadcasted_iota(I32, (N_EXPERTS, tt), 0)
    oh1 = sub64 == e1
    oh2 = sub64 == e2
    oh = oh1.astype(F32) + oh2.astype(F32)
    pref = jnp.dot(oh.astype(BF16), u_ref[...], preferred_element_type=F32)
    excl = carry[...] + pref - 1.0
    rank1 = jnp.sum(jnp.where(oh1, excl, 0.0), axis=0, keepdims=True)
    rank2 = jnp.sum(jnp.where(oh2, excl, 0.0), axis=0, keepdims=True)
    carry[...] = carry[...] + pref[:, tt - 1:tt]
    zi = jnp.zeros((4, tt), I32)
    oi_ref[...] = jnp.concatenate([e1, e2, rank1.astype(I32), rank2.astype(I32), zi], axis=0)
    of_ref[...] = jnp.concatenate([w1, w2, jnp.zeros((6, tt), F32)], axis=0)
    cnt_ref[...] = jnp.broadcast_to(carry[...], (N_EXPERTS, LANES))


def _route(lg_t, bias_col, utri):
    t = lg_t.shape[1]
    tt = utri.shape[0]
    return pl.pallas_call(
        functools.partial(_route_body, tt=tt),
        grid=(t // tt,),
        in_specs=[pl.BlockSpec((LANES, tt), lambda i: (0, i)),
                  pl.BlockSpec((LANES, 1), lambda i: (0, 0)),
                  pl.BlockSpec((tt, tt), lambda i: (0, 0))],
        out_specs=[pl.BlockSpec((8, tt), lambda i: (0, i)),
                   pl.BlockSpec((8, tt), lambda i: (0, i)),
                   pl.BlockSpec((N_EXPERTS, LANES), lambda i: (0, 0))],
        out_shape=[jax.ShapeDtypeStruct((8, t), I32),
                   jax.ShapeDtypeStruct((8, t), F32),
                   jax.ShapeDtypeStruct((N_EXPERTS, LANES), F32)],
        scratch_shapes=[pltpu.VMEM((N_EXPERTS, 1), F32)],
        compiler_params=_cparams(("arbitrary",)),
        name="route",
    )(lg_t, bias_col, utri)


def _dest_body(oi_ref, poff_ref, d_ref, *, tt):
    sub64 = lax.broadcasted_iota(I32, (N_EXPERTS, tt), 0)
    oi = oi_ref[...]
    poff = poff_ref[...]
    d1 = jnp.sum(jnp.where(sub64 == oi[0:1], poff, 0), axis=0, keepdims=True) + oi[2:3]
    d2 = jnp.sum(jnp.where(sub64 == oi[1:2], poff, 0), axis=0, keepdims=True) + oi[3:4]
    d_ref[...] = jnp.concatenate([d1, d2, jnp.zeros((6, tt), I32)], axis=0)


def _dest(oi, poff_col):
    t = oi.shape[1]
    tt = min(t, 2048)
    return pl.pallas_call(
        functools.partial(_dest_body, tt=tt),
        grid=(t // tt,),
        in_specs=[pl.BlockSpec((8, tt), lambda i: (0, i)),
                  pl.BlockSpec((N_EXPERTS, 1), lambda i: (0, 0))],
        out_specs=pl.BlockSpec((8, tt), lambda i: (0, i)),
        out_shape=jax.ShapeDtypeStruct((8, t), I32),
        compiler_params=_cparams(("arbitrary",)),
        name="dest_rows",
    )(oi, poff_col)


def _rowtok_body(d1_ref, d2_ref, rt_ref, *, t, r_alloc):
    def clear(i, c):
        rt_ref[i] = 0
        return c

    lax.fori_loop(0, r_alloc, clear, 0, unroll=8)

    def put(i, c):
        rt_ref[d1_ref[i]] = i
        rt_ref[d2_ref[i]] = i
        return c

    lax.fori_loop(0, t, put, 0, unroll=8)


def _rowtok(d1, d2, r_alloc):
    t = d1.shape[0]
    smem = pl.BlockSpec(memory_space=pltpu.SMEM)
    return pl.pallas_call(
        functools.partial(_rowtok_body, t=t, r_alloc=r_alloc),
        in_specs=[smem, smem],
        out_specs=smem,
        out_shape=jax.ShapeDtypeStruct((r_alloc,), I32),
        name="row_tokens",
    )(d1, d2)


def _experts_body(rt_ref, ie_ref, ir_ref, int_ref, ifl_ref, ni_ref, bi_ref, x_hbm, wg_ref, wu_ref, wdl_ref, wdh_ref,
                  y_hbm, xlo, xhi, hid, ybuf, ring, gp_ref, gsem, ysem, *, nsub):
    w = pl.program_id(0)
    s = pl.program_id(1)
    nt = int_ref[w]
    nfill = ifl_ref[w]
    row0 = ir_ref[w]
    tile0 = row0 // MOE_TM
    n_items = ni_ref[0]
    half = D_MODEL // 2
    hn = D_EXPERT // nsub

    def gather_group():
        g = gp_ref[0]
        base = g * GATHER_GROUP
        slot = (g // GROUPS_PER_TILE) % RING_TILES
        r0 = (g % GROUPS_PER_TILE) * GATHER_GROUP
        for j in range(GATHER_GROUP):
            tok = rt_ref[base + j]
            pltpu.make_async_copy(x_hbm.at[pl.ds(tok, 1)], ring.at[slot, pl.ds(r0 + j, 1)], gsem.at[slot]).start()
        gp_ref[0] = g + 1

    def ring_wait(tile):
        slot = tile % RING_TILES
        pltpu.make_async_copy(x_hbm.at[pl.ds(0, MOE_TM)], ring.at[slot], gsem.at[slot]).wait()

    def tile_copy_out(r0, tl):
        rs = pl.ds(pl.multiple_of(tl * MOE_TM, MOE_TM), MOE_TM)
        dst = pl.ds(pl.multiple_of(r0 + tl * MOE_TM, MOE_TM), MOE_TM)
        return pltpu.make_async_copy(ybuf.at[rs], y_hbm.at[dst], ysem)

    def for_tiles(n, fn):
        def body(tl, c):
            fn(tl)
            return c

        lax.fori_loop(0, n, body, 0)

    @pl.when(jnp.logical_and(w == 0, s == 0))
    def _():
        gp_ref[0] = 0

        def first(i, c):
            gather_group()
            return c

        lax.fori_loop(0, ITEM_TILES * GROUPS_PER_TILE, first, 0)

    @pl.when(jnp.logical_and(nt > 0, s == 0))
    def _():
        for_tiles(nt, lambda tl: ring_wait(tile0 + tl))

    def gateup(tl, step):
        rs = pl.ds(pl.multiple_of(tl * MOE_TM, MOE_TM), MOE_TM)
        if step == 0:
            lo, hi = _unpack_bf16_pair(ring[(tile0 + tl) % RING_TILES])
            a = lo.astype(BF16)
            b = hi.astype(BF16)
            xlo[rs, :] = a
            xhi[rs, :] = b
        else:
            gather_group()
            a = xlo[rs, :]
            b = xhi[rs, :]
        gate = (jnp.dot(a, wg_ref[0:half, :].astype(BF16), preferred_element_type=F32)
                + jnp.dot(b, wg_ref[half:D_MODEL, :].astype(BF16), preferred_element_type=F32))
        up = (jnp.dot(a, wu_ref[0:half, :].astype(BF16), preferred_element_type=F32)
              + jnp.dot(b, wu_ref[half:D_MODEL, :].astype(BF16), preferred_element_type=F32))
        hid[rs, step * hn:(step + 1) * hn] = (_silu(gate) * up).astype(BF16)
        if step == 0:
            gather_group()

    def down(tl, step):
        rs = pl.ds(pl.multiple_of(tl * MOE_TM, MOE_TM), MOE_TM)
        gather_group()
        h = hid[rs, :]
        ybuf[rs, step * hn:(step + 1) * hn] = _pack_bf16_pair(
            jnp.dot(h, wdl_ref[...].astype(BF16), preferred_element_type=F32),
            jnp.dot(h, wdh_ref[...].astype(BF16), preferred_element_type=F32))

    for step in range(nsub):
        @pl.when(jnp.logical_and(nt > 0, s == step))
        def _(step=step):
            for_tiles(nt, lambda tl: gateup(tl, step))

    @pl.when(jnp.logical_and(jnp.logical_and(nt > 0, w > 0), s == nsub))
    def _():
        for_tiles(int_ref[jnp.maximum(w - 1, 0)], lambda tl: tile_copy_out(row0, tl).wait())

    for step in range(nsub):
        @pl.when(jnp.logical_and(nt > 0, s == nsub + step))
        def _(step=step):
            for_tiles(nt, lambda tl: down(tl, step))

    @pl.when(jnp.logical_and(nt > 0, s == 2 * nsub - 1))
    def _():
        for_tiles(nt, lambda tl: tile_copy_out(row0, tl).start())

    @pl.when(jnp.logical_and(w == n_items - 1, s == 2 * nsub - 1))
    def _():
        for_tiles(nt, lambda tl: tile_copy_out(row0, tl).wait())
        for_tiles(ITEM_TILES, lambda tl: ring_wait(tile0 + nt + tl))

    @pl.when(jnp.logical_and(nfill > 0, s == 0))
    def _():
        ybuf[...] = jnp.zeros_like(ybuf)
        for_tiles(nfill, lambda tl: tile_copy_out(row0, tl).start())
        for_tiles(nfill, lambda tl: tile_copy_out(row0, tl).wait())


ITEM_TILES = 4
RING_TILES = 2 * ITEM_TILES
GATHER_GROUP = 64
GROUPS_PER_TILE = MOE_TM // GATHER_GROUP


def _experts(row_tok, item_e, item_row0, item_nt, item_fill, n_items, blk_idx, hn2p, w_gate, w_up, w_down, r_alloc):
    ni = item_e.shape[0]
    nsub = 2
    assert GROUPS_PER_TILE == 2 * nsub
    hn = D_EXPERT // nsub
    rows = ITEM_TILES * MOE_TM
    nstep = 2 * nsub

    ns = ni * nstep

    def gu_map(w, s, rt, ie, ir, nt, fl, n, bi):
        return (bi[w * nstep + s], 0, bi[ns + w * nstep + s])

    def dl_map(w, s, rt, ie, ir, nt, fl, n, bi):
        return (bi[2 * ns + w * nstep + s], 0, bi[3 * ns + w * nstep + s])

    def dh_map(w, s, rt, ie, ir, nt, fl, n, bi):
        return (bi[2 * ns + w * nstep + s], 0, nsub + bi[3 * ns + w * nstep + s])

    return pl.pallas_call(
        functools.partial(_experts_body, nsub=nsub),
        grid_spec=pltpu.PrefetchScalarGridSpec(
            num_scalar_prefetch=7,
            grid=(ni, nstep),
            in_specs=[pl.BlockSpec(memory_space=pl.ANY),
                      pl.BlockSpec((None, D_MODEL, hn), gu_map),
                      pl.BlockSpec((None, D_MODEL, hn), gu_map),
                      pl.BlockSpec((None, D_EXPERT, hn), dl_map),
                      pl.BlockSpec((None, D_EXPERT, hn), dh_map)],
            out_specs=pl.BlockSpec(memory_space=pl.ANY),
            scratch_shapes=[pltpu.VMEM((rows, D_MODEL // 2), BF16),
                            pltpu.VMEM((rows, D_MODEL // 2), BF16),
                            pltpu.VMEM((rows, D_EXPERT), BF16),
                            pltpu.VMEM((rows, D_MODEL // 2), U32),
                            pltpu.VMEM((RING_TILES, MOE_TM, D_MODEL // 2), U32),
                            pltpu.SMEM((1,), I32),
                            pltpu.SemaphoreType.DMA((RING_TILES,)),
                            pltpu.SemaphoreType.DMA(())]),
        out_shape=jax.ShapeDtypeStruct((r_alloc, D_MODEL // 2), U32),
        compiler_params=_cparams(("arbitrary", "arbitrary")),
        name="experts",
    )(row_tok, item_e, item_row0, item_nt, item_fill, n_items, blk_idx, hn2p, w_gate, w_up, w_down, w_down)


def _combine_body(d1_ref, d2_ref, h_ref, w_ref, y_ref, o_ref, buf, sem, *, tt, n_steps):
    i = pl.program_id(0)

    def copies(step, slot, r):
        t0 = step * tt + r
        c1 = pltpu.make_async_copy(y_ref.at[pl.ds(d1_ref[t0], 1)], buf.at[slot, 0, pl.ds(r, 1)], sem.at[slot])
        c2 = pltpu.make_async_copy(y_ref.at[pl.ds(d2_ref[t0], 1)], buf.at[slot, 1, pl.ds(r, 1)], sem.at[slot])
        return c1, c2

    def issue(step, slot):
        def body(r, c):
            c1, c2 = copies(step, slot, r)
            c1.start()
            c2.start()
            return c

        lax.fori_loop(0, tt, body, 0, unroll=8)

    @pl.when(i == 0)
    def _():
        issue(0, 0)

    @pl.when(i + 1 < n_steps)
    def _():
        issue(i + 1, (i + 1) % 2)

    slot = i % 2
    pltpu.make_async_copy(y_ref.at[pl.ds(0, tt)], buf.at[slot, 0], sem.at[slot]).wait()
    pltpu.make_async_copy(y_ref.at[pl.ds(0, tt)], buf.at[slot, 1], sem.at[slot]).wait()
    a_lo, a_hi = _unpack_bf16_pair(buf[slot, 0])
    b_lo, b_hi = _unpack_bf16_pair(buf[slot, 1])
    w1 = w_ref[:, 0:1]
    w2 = w_ref[:, 1:2]
    half = D_MODEL // 2
    o_ref[:, 0:half] = h_ref[:, 0:half] + (a_lo * w1 + b_lo * w2)
    o_ref[:, half:D_MODEL] = h_ref[:, half:D_MODEL] + (a_hi * w1 + b_hi * w2)


def _combine(d1, d2, h1, w_tok, y_rows):
    t = h1.shape[0]
    tt = 256
    n_steps = t // tt
    return pl.pallas_call(
        functools.partial(_combine_body, tt=tt, n_steps=n_steps),
        grid_spec=pltpu.PrefetchScalarGridSpec(
            num_scalar_prefetch=2,
            grid=(n_steps,),
            in_specs=[pl.BlockSpec((tt, D_MODEL), lambda i, d1, d2: (i, 0)),
                      pl.BlockSpec((tt, LANES), lambda i, d1, d2: (i, 0)),
                      pl.BlockSpec(memory_space=pl.ANY)],
            out_specs=pl.BlockSpec((tt, D_MODEL), lambda i, d1, d2: (i, 0)),
            scratch_shapes=[pltpu.VMEM((2, 2, tt, D_MODEL // 2), U32),
                            pltpu.SemaphoreType.DMA((2,))]),
        out_shape=jax.ShapeDtypeStruct((t, D_MODEL), F32),
        compiler_params=_cparams(("arbitrary",)),
        name="moe_combine",
    )(d1, d2, h1, w_tok, y_rows)


def _band_bias():
    qi = np.arange(ATTN_SPAN)[:, None]
    kj = np.arange(ATTN_SPAN)[None, :]
    prev = np.where(kj >= qi, 0.0, NEG)
    own = np.where(kj <= qi, 0.0, NEG)
    with_prev = np.concatenate([prev, own], axis=1)
    no_prev = np.concatenate([np.full_like(prev, NEG), own], axis=1)
    both = np.stack([np.tile(with_prev, (2, 1)), np.tile(no_prev, (2, 1))])
    return jnp.asarray(both, F32)


def kernel(x, positions, norm1_w, w_in, conv_w, conv_b, dt_bias, A_log, D_skip, ssd_norm_w, q_norm_w, k_norm_w,
           w_out, norm2_w, w_group_router, b_group_router, w_expert_router, b_expert_router, w_gate, w_up, w_down):
    b, s, _ = x.shape
    assert b == 1 and norm1_w.shape[0] == 1
    t = s
    x2 = x.reshape(t, D_MODEL)

    w_in0 = w_in[0]
    zc, xc_, bc_, cc_, dtc, qc, kc, vc = np.cumsum((0, 2048, 2048, 512, 512, 32, 1024, 1024))
    w_main = jnp.concatenate([w_in0[:, :dtc].astype(BF16), w_in0[:, qc:].astype(BF16)], axis=1)
    w_dt = jnp.pad(w_in0[:, dtc:qc], ((0, 0), (0, LANES - SSD_HEADS))).astype(BF16)
    pad_h = lambda v: jnp.pad(v.astype(F32), (0, LANES - SSD_HEADS)).reshape(1, LANES)
    lane = np.arange(LANES)
    c64 = lane % ATTN_HEAD_DIM
    inv_freq = 1.0 / (ROPE_THETA ** (jnp.arange(0, ROPE_DIM, 2, dtype=F32) / ROPE_DIM))
    f_lane = jnp.where(jnp.asarray(c64 < ROPE_DIM), inv_freq[jnp.asarray(c64 % (ROPE_DIM // 2))], 0.0).reshape(1, LANES)
    sg_lane = jnp.asarray(np.where(c64 < ROPE_DIM // 2, -1.0, np.where(c64 < ROPE_DIM, 1.0, 0.0)), F32).reshape(1, LANES)
    hsum = jnp.asarray((lane[:, None] // ATTN_HEAD_DIM) == (lane[None, :] // ATTN_HEAD_DIM), BF16)
    hexp = jnp.asarray(np.arange(LANES)[:, None] == (np.arange(SSD_D_INNER)[None, :] // SSD_HEAD_DIM), F32)
    qw_lane = jnp.tile(q_norm_w[0].astype(F32), 2).reshape(1, LANES)
    kw_lane = jnp.tile(k_norm_w[0].astype(F32), 2).reshape(1, LANES)
    dsk = jnp.repeat(D_skip[0].astype(F32), SSD_HEAD_DIM).reshape(1, SSD_D_INNER)
    wr = jnp.concatenate([w_group_router[0],
                          jnp.transpose(w_expert_router[0], (1, 0, 2)).reshape(D_MODEL, N_EXPERTS)], axis=1)
    wr_t = jnp.pad(wr.T.astype(F32), ((0, LANES - N_GROUPS - N_EXPERTS), (0, 0)))
    br = jnp.pad(jnp.concatenate([b_group_router[0], b_expert_router[0].reshape(-1)]).astype(F32),
                 (0, LANES - N_GROUPS - N_EXPERTS)).reshape(LANES, 1)

    cos_t, sin_t = _rope_table(positions.reshape(t, 1), f_lane, sg_lane)
    proj, dtraw = _inproj(x2, norm1_w.astype(F32), w_main, w_dt)
    y_ssd = _ssd(proj, dtraw, conv_w[0].astype(F32), conv_b.astype(F32), pad_h(dt_bias[0]), pad_h(A_log[0]),
                 dsk, ssd_norm_w.astype(F32), hexp)
    y_attn = _attention(proj, cos_t, sin_t, qw_lane, kw_lane, hsum, _band_bias())
    h1, hn2p, lg_t = _outproj(y_ssd, y_attn, x2, w_out[0].astype(BF16), norm2_w.astype(F32), wr_t)

    rt_tt = 512
    utri = jnp.asarray(np.arange(rt_tt)[:, None] <= np.arange(rt_tt)[None, :], BF16)
    oi, of, cnt = _route(lg_t, br, utri)
    counts = cnt[:, 0].astype(I32)
    n_tiles_max = (2 * t + N_EXPERTS * (MOE_TM - 1) + MOE_TM - 1) // MOE_TM
    r_alloc = n_tiles_max * MOE_TM
    tiles_e = (counts + MOE_TM - 1) // MOE_TM
    tile_end = jnp.cumsum(tiles_e)
    n_used = tile_end[-1]
    poff = ((tile_end - tiles_e) * MOE_TM).reshape(N_EXPERTS, 1)
    nt = n_used.reshape(1).astype(I32)
    dst = _dest(oi, poff.astype(I32))
    row_tok = _rowtok(dst[0], dst[1], r_alloc + ITEM_TILES * MOE_TM)

    items_e = (tiles_e + ITEM_TILES - 1) // ITEM_TILES
    item_end = jnp.cumsum(items_e)
    n_items = item_end[-1]
    ni_max = (n_tiles_max + (ITEM_TILES - 1) * N_EXPERTS + ITEM_TILES - 1) // ITEM_TILES
    w_idx = jnp.arange(ni_max, dtype=I32)
    w_live = w_idx < n_items
    w_c = jnp.minimum(w_idx, n_items - 1)
    item_e = jnp.searchsorted(item_end, w_c, side="right").astype(I32)
    k_in_e = w_c - (item_end - items_e)[item_e]
    spare = w_idx - n_items
    item_row0 = jnp.where(w_live, poff[item_e, 0] + k_in_e * (ITEM_TILES * MOE_TM),
                          (n_used + ITEM_TILES * spare) * MOE_TM).astype(I32)
    item_nt = jnp.where(w_live, jnp.clip(tiles_e[item_e] - ITEM_TILES * k_in_e, 0, ITEM_TILES), 0).astype(I32)
    item_fill = jnp.where(w_live, 0,
                          jnp.clip(n_tiles_max - n_used - ITEM_TILES * spare, 0, ITEM_TILES)).astype(I32)

    e_last = item_e[jnp.maximum(n_items - 1, 0)]
    e_next = jnp.where(w_idx + 1 < n_items, item_e[jnp.minimum(w_idx + 1, ni_max - 1)], e_last)
    e_prev = item_e[jnp.maximum(w_idx - 1, 0)]
    gu_e = jnp.stack([item_e, item_e, e_next, e_next], axis=1)
    gu_h = jnp.broadcast_to(jnp.asarray([0, 1, 0, 0], I32), (ni_max, 4))
    dn_e = jnp.stack([e_prev, e_prev, item_e, item_e], axis=1)
    dn_h = jnp.where((w_idx == 0)[:, None], jnp.asarray([0, 0, 0, 1], I32), jnp.asarray([1, 1, 0, 1], I32))
    dead = jnp.logical_not(w_live)[:, None]
    gu_e = jnp.where(dead, e_last, gu_e)
    gu_h = jnp.where(dead, 0, gu_h)
    dn_e = jnp.where(dead, e_last, dn_e)
    dn_h = jnp.where(dead, 1, dn_h)
    blk_idx = jnp.concatenate([gu_e.reshape(-1), gu_h.reshape(-1), dn_e.reshape(-1), dn_h.reshape(-1)]).astype(I32)

    y_rows = _experts(row_tok, item_e, item_row0, item_nt, item_fill, n_items.reshape(1).astype(I32), blk_idx,
                      hn2p, w_gate[0], w_up[0], w_down[0], r_alloc)
    w_tok = jnp.pad(of[0:2].T, ((0, 0), (0, LANES - 2)))
    out = _combine(dst[0], dst[1], h1, w_tok, y_rows)
    return out.reshape(b, s, D_MODEL)
```

```python
import functools

import jax
import jax.numpy as jnp
import numpy as np
from jax import lax
from jax.experimental import pallas as pl
from jax.experimental.pallas import tpu as pltpu

F32 = jnp.float32
BF16 = jnp.bfloat16
I32 = jnp.int32
U32 = jnp.uint32

D_MODEL = 2048
SSD_HEADS = 32
SSD_HEAD_DIM = 64
SSD_D_INNER = 2048
SSD_GROUPS = 4
SSD_D_STATE = 128
SSD_CONV = 4
SSD_BC = SSD_GROUPS * SSD_D_STATE
SSD_CONV_DIM = SSD_D_INNER + 2 * SSD_BC
ATTN_HEADS = 16
ATTN_HEAD_DIM = 64
ATTN_D = 1024
ATTN_SPAN = 128
DILATIONS = (1, 4, 16)
ROPE_DIM = 16
ROPE_THETA = 500000.0
N_GROUPS = 8
EXPERTS_PER_GROUP = 8
N_EXPERTS = 64
D_EXPERT = 1024
EPS = 1e-6
NEG = -1e30

LANES = 128
VMEM_LIMIT = 56 * 1024 * 1024

COL_Z, COL_XS, COL_B, COL_C, COL_Q, COL_K, COL_V = 0, 2048, 4096, 4608, 5120, 6144, 7168
PROJ_COLS = 8192

SSD_L = 128
ATT_BLK = 2048
MOE_TM = 256


def _cparams(sem):
    return pltpu.CompilerParams(dimension_semantics=sem, vmem_limit_bytes=VMEM_LIMIT)


def _silu(v):
    return v * (1.0 / (1.0 + jnp.exp(-v)))


def _rope_body(pos_ref, f_ref, sg_ref, cos_ref, sin_ref):
    ang = pos_ref[...].astype(F32) * f_ref[...]
    cos_ref[...] = jnp.cos(ang)
    sin_ref[...] = jnp.sin(ang) * sg_ref[...]


def _rope_table(pos_col, f_lane, sg_lane):
    t = pos_col.shape[0]
    tt = min(t, 2048)
    return pl.pallas_call(
        _rope_body,
        grid=(t // tt,),
        in_specs=[pl.BlockSpec((tt, 1), lambda i: (i, 0)),
                  pl.BlockSpec((1, LANES), lambda i: (0, 0)),
                  pl.BlockSpec((1, LANES), lambda i: (0, 0))],
        out_specs=[pl.BlockSpec((tt, LANES), lambda i: (i, 0))] * 2,
        out_shape=[jax.ShapeDtypeStruct((t, LANES), F32)] * 2,
        compiler_params=_cparams(("arbitrary",)),
        name="rope_table",
    )(pos_col, f_lane, sg_lane)


def _inproj_body(x_ref, nw_ref, w_ref, wdt_ref, proj_ref, dt_ref, hn_ref, *, tm, rc):
    j = pl.program_id(1)

    @pl.when(j == 0)
    def _():
        def chunk(c, carry):
            r = pl.multiple_of(c * rc, rc)
            xf = x_ref[pl.ds(r, rc), :]
            ms = jnp.mean(xf * xf, axis=-1, keepdims=True)
            hn_ref[pl.ds(r, rc), :] = (xf * lax.rsqrt(ms + EPS) * nw_ref[...]).astype(BF16)
            return carry

        lax.fori_loop(0, tm // rc, chunk, 0)
        dt_ref[...] = jnp.dot(hn_ref[...], wdt_ref[...], preferred_element_type=F32)

    proj_ref[...] = jnp.dot(hn_ref[...], w_ref[...], preferred_element_type=F32).astype(BF16)


def _inproj(x2, nw, w, wdt):
    t = x2.shape[0]
    tm, tn = min(t, 1024), 1024
    return pl.pallas_call(
        functools.partial(_inproj_body, tm=tm, rc=128),
        grid=(t // tm, PROJ_COLS // tn),
        in_specs=[pl.BlockSpec((tm, D_MODEL), lambda i, j: (i, 0)),
                  pl.BlockSpec((1, D_MODEL), lambda i, j: (0, 0)),
                  pl.BlockSpec((D_MODEL, tn), lambda i, j: (0, j)),
                  pl.BlockSpec((D_MODEL, LANES), lambda i, j: (0, 0))],
        out_specs=[pl.BlockSpec((tm, tn), lambda i, j: (i, j)),
                   pl.BlockSpec((tm, LANES), lambda i, j: (i, 0))],
        out_shape=[jax.ShapeDtypeStruct((t, PROJ_COLS), BF16),
                   jax.ShapeDtypeStruct((t, LANES), F32)],
        scratch_shapes=[pltpu.VMEM((tm, D_MODEL), BF16)],
        compiler_params=_cparams(("arbitrary", "arbitrary")),
        name="inproj",
    )(x2, nw, w, wdt)


def _ssd_body(z_ref, xs_ref, b_ref, c_ref, dtr_ref, cw_ref, cb_ref, dtb_ref, alog_ref, dsk_ref, nw_ref,
              hexp_ref, shift_ref, y_ref, ubuf, xc, bc, cc, ybuf, s_ref, *, L):
    ci = pl.program_id(0)

    @pl.when(ci == 0)
    def _():
        ubuf[0:CONV_HALO, :] = jnp.zeros((CONV_HALO, SSD_CONV_DIM), BF16)
        s_ref[...] = jnp.zeros_like(s_ref)

    ubuf[CONV_HALO:CONV_HALO + L, 0:SSD_D_INNER] = xs_ref[...]
    ubuf[CONV_HALO:CONV_HALO + L, SSD_D_INNER:SSD_D_INNER + SSD_BC] = b_ref[...]
    ubuf[CONV_HALO:CONV_HALO + L, SSD_D_INNER + SSD_BC:SSD_CONV_DIM] = c_ref[...]
    cw = 512
    for cch in range(SSD_CONV_DIM // cw):
        cs_ = slice(cw * cch, cw * cch + cw)
        taps = jnp.dot(shift_ref[...], ubuf[:, cs_], preferred_element_type=F32)
        acc = cb_ref[:, cs_] + cw_ref[0:1, cs_] * taps[0:L]
        for k in range(1, SSD_CONV):
            acc = acc + cw_ref[k:k + 1, cs_] * taps[k * L:(k + 1) * L]
        act = _silu(acc)
        if cch < 4:
            xc[:, cs_] = act
        elif cch == 4:
            bc[...] = act
        else:
            cc[...] = act
    ubuf[0:CONV_HALO, :] = ubuf[L:L + CONV_HALO, :]

    dt_in = dtr_ref[...] + dtb_ref[...]
    dt = jnp.maximum(dt_in, 0.0) + jnp.log1p(jnp.exp(-jnp.abs(dt_in)))
    a = dt * (-jnp.exp(alog_ref[...]))
    row = lax.broadcasted_iota(I32, (L, L), 0)
    col = lax.broadcasted_iota(I32, (L, L), 1)
    causal = col <= row
    cs = jnp.dot(causal.astype(F32), a, precision=lax.Precision.HIGHEST, preferred_element_type=F32)
    cs_last = cs[L - 1:L, :]
    wmat = jnp.exp(cs_last - cs) * dt
    cs_t = cs.T
    dt_t = dt.T
    cdec = jnp.broadcast_to(jnp.exp(cs_last), (16, LANES))
    c1 = cdec.astype(BF16)
    r1 = cdec - c1.astype(F32)
    c2 = r1.astype(BF16)
    c3 = (r1 - c2.astype(F32)).astype(BF16)
    cdec_x = (jnp.dot(c1, hexp_ref[...], preferred_element_type=F32)
              + jnp.dot(c2, hexp_ref[...], preferred_element_type=F32)
              + jnp.dot(c3, hexp_ref[...], preferred_element_type=F32))[0:1, :]
    lane = lax.broadcasted_iota(I32, (L, LANES), 1)
    first = lane < SSD_HEAD_DIM

    for g in range(SSD_GROUPS):
        bg = bc[:, LANES * g:LANES * g + LANES]
        cg = cc[:, LANES * g:LANES * g + LANES]
        cb = lax.dot_general(cg.astype(BF16), bg.astype(BF16), (((1,), (1,)), ((), ())),
                             preferred_element_type=F32)
        bg_t = bg.T.astype(BF16)
        xw_parts = []
        for q in range(4):
            lanes_ = slice(512 * g + LANES * q, 512 * g + LANES * q + LANES)
            x_pair = xc[:, lanes_]
            s_pair = s_ref[g, :, LANES * q:LANES * q + LANES]
            rhs = jnp.concatenate([x_pair.astype(BF16), s_pair.astype(BF16)], axis=0)
            ys, wbs = [], []
            for e2 in range(2):
                h = 8 * g + 2 * q + e2
                cs_col = jnp.broadcast_to(cs[:, h:h + 1], (L, L))
                lm = jnp.exp(jnp.where(causal, cs_col - cs_t[h:h + 1, :], NEG))
                m = cb * lm * dt_t[h:h + 1, :]
                e_col = jnp.exp(jnp.broadcast_to(cs[:, h:h + 1], (L, LANES)))
                lhs = jnp.concatenate([m.astype(BF16), (cg * e_col).astype(BF16)], axis=1)
                ys.append(jnp.dot(lhs, rhs, preferred_element_type=F32))
                wbs.append(jnp.broadcast_to(wmat[:, h:h + 1], (L, LANES)))
            ybuf[:, lanes_] = jnp.where(first, ys[0], ys[1])
            xw_parts.append((x_pair * jnp.where(first, wbs[0], wbs[1])).astype(BF16))
        xw_g = jnp.concatenate(xw_parts, axis=1)
        s_ref[g] = (s_ref[g] * cdec_x[:, 512 * g:512 * g + 512]
                    + jnp.dot(bg_t, xw_g, preferred_element_type=F32))

    for g in range(SSD_GROUPS):
        gs = slice(512 * g, 512 * g + 512)
        zf = z_ref[:, gs].astype(F32)
        yg = (ybuf[:, gs] + dsk_ref[:, gs] * xc[:, gs]) * _silu(zf)
        ms = jnp.mean(yg * yg, axis=-1, keepdims=True)
        y_ref[:, gs] = (yg * lax.rsqrt(ms + EPS) * nw_ref[:, gs]).astype(BF16)


CONV_HALO = 16


def _conv_shift_matrix(L):
    s = np.zeros((SSD_CONV * L, CONV_HALO + L), np.float32)
    for k in range(SSD_CONV):
        s[k * L + np.arange(L), CONV_HALO + np.arange(L) - (SSD_CONV - 1) + k] = 1.0
    return jnp.asarray(s, BF16)


def _ssd(proj, dtraw, cw, cb, dtb, alog, dsk, nw, hexp):
    t = proj.shape[0]
    L = SSD_L
    shift = _conv_shift_matrix(L)
    full = lambda shape: pl.BlockSpec(shape, lambda i: (0,) * len(shape))
    return pl.pallas_call(
        functools.partial(_ssd_body, L=L),
        grid=(t // L,),
        in_specs=[pl.BlockSpec((L, SSD_D_INNER), lambda i: (i, COL_Z // SSD_D_INNER)),
                  pl.BlockSpec((L, SSD_D_INNER), lambda i: (i, COL_XS // SSD_D_INNER)),
                  pl.BlockSpec((L, SSD_BC), lambda i: (i, COL_B // SSD_BC)),
                  pl.BlockSpec((L, SSD_BC), lambda i: (i, COL_C // SSD_BC)),
                  pl.BlockSpec((L, LANES), lambda i: (i, 0)),
                  full((SSD_CONV, SSD_CONV_DIM)), full((1, SSD_CONV_DIM)),
                  full((1, LANES)), full((1, LANES)),
                  full((1, SSD_D_INNER)), full((1, SSD_D_INNER)), full((LANES, SSD_D_INNER)),
                  full((SSD_CONV * L, CONV_HALO + L))],
        out_specs=pl.BlockSpec((L, SSD_D_INNER), lambda i: (i, 0)),
        out_shape=jax.ShapeDtypeStruct((t, SSD_D_INNER), BF16),
        scratch_shapes=[pltpu.VMEM((CONV_HALO + L, SSD_CONV_DIM), BF16),
                        pltpu.VMEM((L, SSD_D_INNER), F32),
                        pltpu.VMEM((L, SSD_BC), F32),
                        pltpu.VMEM((L, SSD_BC), F32),
                        pltpu.VMEM((L, SSD_D_INNER), F32),
                        pltpu.VMEM((SSD_GROUPS, SSD_D_STATE, 512), F32)],
        compiler_params=_cparams(("arbitrary",)),
        name="ssd_scan",
    )(proj, proj, proj, proj, dtraw, cw, cb, dtb, alog, dsk, nw, hexp, shift)


def _attn_body(q_ref, k_ref, v_ref, cos_ref, sin_ref, qw_ref, kw_ref, hsum_ref, bias_ref, o_ref,
               q_s, k_s, v_s, o_b, m_b, l_b, *, blk):
    j = pl.program_id(1)
    cur = (j % 2) * blk
    lane = lax.broadcasted_iota(I32, (1, LANES), 1)
    c64 = lane % ATTN_HEAD_DIM
    low = c64 < (ROPE_DIM // 2)

    def norm_rope(raw, w_lane):
        xf = raw.astype(F32)
        x2 = xf * xf
        hi = x2.astype(BF16)
        lo = (x2 - hi.astype(F32)).astype(BF16)
        ss = (jnp.dot(hi, hsum_ref[...], preferred_element_type=F32)
              + jnp.dot(lo, hsum_ref[...], preferred_element_type=F32))
        y = xf * lax.rsqrt(ss * (1.0 / ATTN_HEAD_DIM) + EPS) * w_lane
        partner = jnp.where(low, pltpu.roll(y, LANES - ROPE_DIM // 2, 1), pltpu.roll(y, ROPE_DIM // 2, 1))
        return y * cos_ref[...] + partner * sin_ref[...]

    @pl.when(j == 0)
    def _():
        k_s[pl.ds(blk, blk), :] = jnp.zeros((blk, LANES), F32)
        v_s[pl.ds(blk, blk), :] = jnp.zeros((blk, LANES), F32)

    q_s[...] = norm_rope(q_ref[...], qw_ref[...]) * (ATTN_HEAD_DIM ** -0.5)
    k_s[pl.ds(cur, blk), :] = norm_rope(k_ref[...], kw_ref[...])
    v_s[pl.ds(cur, blk), :] = v_ref[...].astype(F32)
    lane_q = lax.broadcasted_iota(I32, (ATTN_SPAN, LANES), 1)
    first = lane_q < ATTN_HEAD_DIM
    mask0 = first.astype(F32)
    mask1 = 1.0 - mask0
    ones_v = jnp.ones((2 * ATTN_SPAN, LANES), BF16)

    def rows(ref, start, d):
        if d == 1:
            return ref[pl.ds(start, ATTN_SPAN), :]
        return ref[pl.ds(start, ATTN_SPAN, stride=d), :]

    def tile(br, start, d):
        qv = rows(q_s, start, d)
        qs = jnp.concatenate([qv * mask0, qv * mask1], axis=0).astype(BF16)
        prev_start = (cur + start - ATTN_SPAN * d) & (2 * blk - 1)
        own_start = cur + start
        kt = jnp.concatenate([rows(k_s, prev_start, d), rows(k_s, own_start, d)], axis=0).astype(BF16)
        vt = jnp.concatenate([rows(v_s, prev_start, d), rows(v_s, own_start, d)], axis=0).astype(BF16)
        s = lax.dot_general(qs, kt, (((1,), (1,)), ((), ())), preferred_element_type=F32)
        no_prev = jnp.logical_and(j == 0, start < ATTN_SPAN * d)
        s = s + bias_ref[no_prev.astype(I32)]
        m2 = jnp.max(s, axis=-1, keepdims=True)
        p = jnp.exp(s - m2).astype(BF16)
        o2 = jnp.dot(p, jnp.concatenate([vt, ones_v], axis=1), preferred_element_type=F32)
        m_rep = jnp.broadcast_to(m2, (2 * ATTN_SPAN, LANES))
        if d == 1:
            sl = pl.ds(start, ATTN_SPAN)
        else:
            sl = pl.ds(start, ATTN_SPAN, stride=d)
        o_b[br, sl, :] = jnp.where(first, o2[:ATTN_SPAN, :LANES], o2[ATTN_SPAN:, :LANES])
        l_b[br, sl, :] = jnp.where(first, o2[:ATTN_SPAN, LANES:], o2[ATTN_SPAN:, LANES:])
        m_b[br, sl, :] = jnp.where(first, m_rep[:ATTN_SPAN], m_rep[ATTN_SPAN:])

    for br, d in enumerate(DILATIONS):
        n_res = d
        n_sub = blk // (d * ATTN_SPAN)

        def loop_body(i, carry, br=br, d=d, n_res=n_res):
            r = i % n_res
            sub = i // n_res
            tile(br, r + sub * (d * ATTN_SPAN), d)
            return carry

        lax.fori_loop(0, n_res * n_sub, loop_body, 0, unroll=8)

    def merge(c, carry):
        rs = pl.ds(pl.multiple_of(c * ATTN_SPAN, ATTN_SPAN), ATTN_SPAN)
        ms = [m_b[br, rs, :] for br in range(len(DILATIONS))]
        m_all = jnp.maximum(jnp.maximum(ms[0], ms[1]), ms[2])
        ws = [jnp.exp(m - m_all) for m in ms]
        num = ws[0] * o_b[0, rs, :] + ws[1] * o_b[1, rs, :] + ws[2] * o_b[2, rs, :]
        den = ws[0] * l_b[0, rs, :] + ws[1] * l_b[1, rs, :] + ws[2] * l_b[2, rs, :]
        o_ref[rs, :] = (num / den).astype(BF16)
        return carry

    lax.fori_loop(0, blk // ATTN_SPAN, merge, 0)


def _attention(proj, cos_t, sin_t, qw_lane, kw_lane, hsum, bias):
    t = proj.shape[0]
    blk = ATT_BLK
    n_hp = ATTN_HEADS // 2
    colblk = lambda base: (lambda hp, j: (j, base // LANES + hp))
    full = lambda shape: pl.BlockSpec(shape, lambda hp, j: (0,) * len(shape))
    return pl.pallas_call(
        functools.partial(_attn_body, blk=blk),
        grid=(n_hp, t // blk),
        in_specs=[pl.BlockSpec((blk, LANES), colblk(COL_Q)),
                  pl.BlockSpec((blk, LANES), colblk(COL_K)),
                  pl.BlockSpec((blk, LANES), colblk(COL_V)),
                  pl.BlockSpec((blk, LANES), lambda hp, j: (j, 0)),
                  pl.BlockSpec((blk, LANES), lambda hp, j: (j, 0)),
                  full((1, LANES)), full((1, LANES)), full((LANES, LANES)),
                  full((2, 2 * ATTN_SPAN, 2 * ATTN_SPAN))],
        out_specs=pl.BlockSpec((blk, LANES), lambda hp, j: (j, hp)),
        out_shape=jax.ShapeDtypeStruct((t, ATTN_D), BF16),
        scratch_shapes=[pltpu.VMEM((blk, LANES), F32),
                        pltpu.VMEM((2 * blk, LANES), F32),
                        pltpu.VMEM((2 * blk, LANES), F32),
                        pltpu.VMEM((len(DILATIONS), blk, LANES), F32),
                        pltpu.VMEM((len(DILATIONS), blk, LANES), F32),
                        pltpu.VMEM((len(DILATIONS), blk, LANES), F32)],
        compiler_params=_cparams(("arbitrary", "arbitrary")),
        name="dilated_attn",
    )(proj, proj, proj, cos_t, sin_t, qw_lane, kw_lane, hsum, bias)


def _pack_bf16_pair(lo_f32, hi_f32):
    lo_bits = pltpu.bitcast(lo_f32.astype(BF16).astype(F32), U32)
    hi_bits = pltpu.bitcast(hi_f32.astype(BF16).astype(F32), U32)
    return (lo_bits >> 16) | (hi_bits & jnp.uint32(0xFFFF0000))


def _unpack_bf16_pair(w):
    lo = pltpu.bitcast(w << 16, F32)
    hi = pltpu.bitcast(w & jnp.uint32(0xFFFF0000), F32)
    return lo, hi


def _outproj_body(ys_ref, ya_ref, x_ref, w_ref, nw_ref, wr_ref, h_ref, hn_ref, lg_ref):
    acc = jnp.dot(ys_ref[...], w_ref[0:SSD_D_INNER, :], preferred_element_type=F32)
    acc = acc + jnp.dot(ya_ref[...], w_ref[SSD_D_INNER:SSD_D_INNER + ATTN_D, :], preferred_element_type=F32)
    h = x_ref[...] + acc
    h_ref[...] = h
    ms = jnp.mean(h * h, axis=-1, keepdims=True)
    hn = h * lax.rsqrt(ms + EPS) * nw_ref[...]
    half = D_MODEL // 2
    hn_ref[...] = _pack_bf16_pair(hn[:, :half], hn[:, half:])
    hn_hi = hn.astype(BF16)
    hn_lo = (hn - hn_hi.astype(F32)).astype(BF16)
    nt_dims = (((1,), (1,)), ((), ()))
    lg_ref[...] = (lax.dot_general(wr_ref[0], hn_hi, nt_dims, preferred_element_type=F32)
                   + lax.dot_general(wr_ref[0], hn_lo, nt_dims, preferred_element_type=F32)
                   + lax.dot_general(wr_ref[1], hn_hi, nt_dims, preferred_element_type=F32))


def _outproj(yssd, yattn, x2, w, nw, wr_t):
    t = x2.shape[0]
    tm = 512
    full = lambda shape: pl.BlockSpec(shape, lambda i: (0,) * len(shape))
    return pl.pallas_call(
        _outproj_body,
        grid=(t // tm,),
        in_specs=[pl.BlockSpec((tm, SSD_D_INNER), lambda i: (i, 0)),
                  pl.BlockSpec((tm, ATTN_D), lambda i: (i, 0)),
                  pl.BlockSpec((tm, D_MODEL), lambda i: (i, 0)),
                  full((SSD_D_INNER + ATTN_D, D_MODEL)), full((1, D_MODEL)), full((2, LANES, D_MODEL))],
        out_specs=[pl.BlockSpec((tm, D_MODEL), lambda i: (i, 0)),
                   pl.BlockSpec((tm, D_MODEL // 2), lambda i: (i, 0)),
                   pl.BlockSpec((LANES, tm), lambda i: (0, i))],
        out_shape=[jax.ShapeDtypeStruct((t, D_MODEL), F32),
                   jax.ShapeDtypeStruct((t, D_MODEL // 2), U32),
                   jax.ShapeDtypeStruct((LANES, t), F32)],
        compiler_params=_cparams(("arbitrary",)),
        name="outproj",
    )(yssd, yattn, x2, w, nw, wr_t)


def _route_body(lg_ref, bias_ref, u_ref, oi_ref, of_ref, cnt_ref, carry, *, tt):
    i = pl.program_id(0)

    @pl.when(i == 0)
    def _():
        carry[...] = jnp.zeros_like(carry)

    lg = lg_ref[...] + bias_ref[...]
    sub8 = lax.broadcasted_iota(I32, (8, tt), 0)
    g = lg[0:8]
    gmax = jnp.max(g, axis=0, keepdims=True)
    gidx = jnp.min(jnp.where(g == gmax, sub8, 8), axis=0, keepdims=True)
    gval = 1.0 / jnp.sum(jnp.exp(g - gmax), axis=0, keepdims=True)
    esel = jnp.zeros((8, tt), F32)
    for grp in range(N_GROUPS):
        esel = jnp.where(gidx == grp, lg[8 + 8 * grp:16 + 8 * grp], esel)
    m1 = jnp.max(esel, axis=0, keepdims=True)
    i1 = jnp.min(jnp.where(esel == m1, sub8, 8), axis=0, keepdims=True)
    em = jnp.where(sub8 == i1, -jnp.inf, esel)
    m2 = jnp.max(em, axis=0, keepdims=True)
    i2 = jnp.min(jnp.where(em == m2, sub8, 8), axis=0, keepdims=True)
    r = jnp.exp(m2 - m1)
    w1 = gval * (1.0 / (1.0 + r))
    w2 = gval * (r / (1.0 + r))
    e1 = gidx * EXPERTS_PER_GROUP + i1
    e2 = gidx * EXPERTS_PER_GROUP + i2
    sub64 = lax.broadcasted_iota(I32, (N_EXPERTS, tt), 0)
    oh1 = sub64 == e1
    oh2 = sub64 == e2
    oh = oh1.astype(F32) + oh2.astype(F32)
    pref = jnp.dot(oh.astype(BF16), u_ref[...], preferred_element_type=F32)
    excl = carry[...] + pref - 1.0
    rank1 = jnp.sum(jnp.where(oh1, excl, 0.0), axis=0, keepdims=True)
    rank2 = jnp.sum(jnp.where(oh2, excl, 0.0), axis=0, keepdims=True)
    carry[...] = carry[...] + pref[:, tt - 1:tt]
    zi = jnp.zeros((4, tt), I32)
    oi_ref[...] = jnp.concatenate([e1, e2, rank1.astype(I32), rank2.astype(I32), zi], axis=0)
    of_ref[...] = jnp.concatenate([w1, w2, jnp.zeros((6, tt), F32)], axis=0)
    cnt_ref[...] = jnp.broadcast_to(carry[...], (N_EXPERTS, LANES))


def _route(lg_t, bias_col, utri):
    t = lg_t.shape[1]
    tt = utri.shape[0]
    return pl.pallas_call(
        functools.partial(_route_body, tt=tt),
        grid=(t // tt,),
        in_specs=[pl.BlockSpec((LANES, tt), lambda i: (0, i)),
                  pl.BlockSpec((LANES, 1), lambda i: (0, 0)),
                  pl.BlockSpec((tt, tt), lambda i: (0, 0))],
        out_specs=[pl.BlockSpec((8, tt), lambda i: (0, i)),
                   pl.BlockSpec((8, tt), lambda i: (0, i)),
                   pl.BlockSpec((N_EXPERTS, LANES), lambda i: (0, 0))],
        out_shape=[jax.ShapeDtypeStruct((8, t), I32),
                   jax.ShapeDtypeStruct((8, t), F32),
                   jax.ShapeDtypeStruct((N_EXPERTS, LANES), F32)],
        scratch_shapes=[pltpu.VMEM((N_EXPERTS, 1), F32)],
        compiler_params=_cparams(("arbitrary",)),
        name="route",
    )(lg_t, bias_col, utri)


def _dest_body(oi_ref, poff_ref, d_ref, *, tt):
    sub64 = lax.broadcasted_iota(I32, (N_EXPERTS, tt), 0)
    oi = oi_ref[...]
    poff = poff_ref[...]
    d1 = jnp.sum(jnp.where(sub64 == oi[0:1], poff, 0), axis=0, keepdims=True) + oi[2:3]
    d2 = jnp.sum(jnp.where(sub64 == oi[1:2], poff, 0), axis=0, keepdims=True) + oi[3:4]
    d_ref[...] = jnp.concatenate([d1, d2, jnp.zeros((6, tt), I32)], axis=0)


def _dest(oi, poff_col):
    t = oi.shape[1]
    tt = min(t, 2048)
    return pl.pallas_call(
        functools.partial(_dest_body, tt=tt),
        grid=(t // tt,),
        in_specs=[pl.BlockSpec((8, tt), lambda i: (0, i)),
                  pl.BlockSpec((N_EXPERTS, 1), lambda i: (0, 0))],
        out_specs=pl.BlockSpec((8, tt), lambda i: (0, i)),
        out_shape=jax.ShapeDtypeStruct((8, t), I32),
        compiler_params=_cparams(("arbitrary",)),
        name="dest_rows",
    )(oi, poff_col)


def _rowtok_body(d1_ref, d2_ref, rt_ref, *, t, r_alloc):
    def clear(i, c):
        rt_ref[i] = 0
        return c

    lax.fori_loop(0, r_alloc, clear, 0, unroll=8)

    def put(i, c):
        rt_ref[d1_ref[i]] = i
        rt_ref[d2_ref[i]] = i
        return c

    lax.fori_loop(0, t, put, 0, unroll=8)


def _rowtok(d1, d2, r_alloc):
    t = d1.shape[0]
    smem = pl.BlockSpec(memory_space=pltpu.SMEM)
    return pl.pallas_call(
        functools.partial(_rowtok_body, t=t, r_alloc=r_alloc),
        in_specs=[smem, smem],
        out_specs=smem,
        out_shape=jax.ShapeDtypeStruct((r_alloc,), I32),
        name="row_tokens",
    )(d1, d2)


def _experts_body(rt_ref, ie_ref, ir_ref, int_ref, ifl_ref, ni_ref, bi_ref, x_hbm,
                  wga_ref, wgb_ref, wua_ref, wub_ref, wdla_ref, wdlb_ref, wdha_ref, wdhb_ref,
                  y_hbm, xlo, xhi, hid, ybuf, ring, gp_ref, gsem, ysem, *, nsub):
    w = pl.program_id(0)
    s = pl.program_id(1)
    nt = int_ref[w]
    nfill = ifl_ref[w]
    row0 = ir_ref[w]
    tile0 = row0 // MOE_TM
    n_items = ni_ref[0]
    half = D_MODEL // 2
    hn = D_EXPERT // nsub

    def gather_group():
        g = gp_ref[0]
        base = g * GATHER_GROUP
        slot = (g // GROUPS_PER_TILE) % RING_TILES
        r0 = (g % GROUPS_PER_TILE) * GATHER_GROUP
        for j in range(GATHER_GROUP):
            tok = rt_ref[base + j]
            pltpu.make_async_copy(x_hbm.at[pl.ds(tok, 1)], ring.at[slot, pl.ds(r0 + j, 1)], gsem.at[slot]).start()
        gp_ref[0] = g + 1

    def ring_wait(tile):
        slot = tile % RING_TILES
        pltpu.make_async_copy(x_hbm.at[pl.ds(0, MOE_TM)], ring.at[slot], gsem.at[slot]).wait()

    def tile_copy_out(r0, tl):
        rs = pl.ds(pl.multiple_of(tl * MOE_TM, MOE_TM), MOE_TM)
        dst = pl.ds(pl.multiple_of(r0 + tl * MOE_TM, MOE_TM), MOE_TM)
        return pltpu.make_async_copy(ybuf.at[rs], y_hbm.at[dst], ysem)

    def for_tiles(n, fn):
        def body(tl, c):
            fn(tl)
            return c

        lax.fori_loop(0, n, body, 0)

    @pl.when(jnp.logical_and(w == 0, s == 0))
    def _():
        gp_ref[0] = 0

        def first(i, c):
            gather_group()
            return c

        lax.fori_loop(0, ITEM_TILES * GROUPS_PER_TILE, first, 0)

    @pl.when(jnp.logical_and(nt > 0, s == 0))
    def _():
        for_tiles(nt, lambda tl: ring_wait(tile0 + tl))

    def gateup(tl, step):
        rs = pl.ds(pl.multiple_of(tl * MOE_TM, MOE_TM), MOE_TM)
        if step == 0:
            lo, hi = _unpack_bf16_pair(ring[(tile0 + tl) % RING_TILES])
            a = lo.astype(BF16)
            b = hi.astype(BF16)
            xlo[rs, :] = a
            xhi[rs, :] = b
        else:
            gather_group()
            a = xlo[rs, :]
            b = xhi[rs, :]
        gate = (jnp.dot(a, wga_ref[...].astype(BF16), preferred_element_type=F32)
                + jnp.dot(b, wgb_ref[...].astype(BF16), preferred_element_type=F32))
        up = (jnp.dot(a, wua_ref[...].astype(BF16), preferred_element_type=F32)
              + jnp.dot(b, wub_ref[...].astype(BF16), preferred_element_type=F32))
        hid[rs, step * hn:(step + 1) * hn] = (_silu(gate) * up).astype(BF16)
        if step == 0:
            gather_group()

    def down(tl, step):
        rs = pl.ds(pl.multiple_of(tl * MOE_TM, MOE_TM), MOE_TM)
        gather_group()
        ha = hid[rs, 0:D_EXPERT // 2]
        hb = hid[rs, D_EXPERT // 2:D_EXPERT]
        ybuf[rs, step * hn:(step + 1) * hn] = _pack_bf16_pair(
            jnp.dot(ha, wdla_ref[...].astype(BF16), preferred_element_type=F32)
            + jnp.dot(hb, wdlb_ref[...].astype(BF16), preferred_element_type=F32),
            jnp.dot(ha, wdha_ref[...].astype(BF16), preferred_element_type=F32)
            + jnp.dot(hb, wdhb_ref[...].astype(BF16), preferred_element_type=F32))

    for step in range(nsub):
        @pl.when(jnp.logical_and(nt > 0, s == step))
        def _(step=step):
            for_tiles(nt, lambda tl: gateup(tl, step))

    @pl.when(jnp.logical_and(jnp.logical_and(nt > 0, w > 0), s == nsub))
    def _():
        for_tiles(int_ref[jnp.maximum(w - 1, 0)], lambda tl: tile_copy_out(row0, tl).wait())

    for step in range(nsub):
        @pl.when(jnp.logical_and(nt > 0, s == nsub + step))
        def _(step=step):
            for_tiles(nt, lambda tl: down(tl, step))

    @pl.when(jnp.logical_and(nt > 0, s == 2 * nsub - 1))
    def _():
        for_tiles(nt, lambda tl: tile_copy_out(row0, tl).start())

    @pl.when(jnp.logical_and(w == n_items - 1, s == 2 * nsub - 1))
    def _():
        for_tiles(nt, lambda tl: tile_copy_out(row0, tl).wait())
        for_tiles(ITEM_TILES, lambda tl: ring_wait(tile0 + nt + tl))

    @pl.when(jnp.logical_and(nfill > 0, s == 0))
    def _():
        ybuf[...] = jnp.zeros_like(ybuf)
        for_tiles(nfill, lambda tl: tile_copy_out(row0, tl).start())
        for_tiles(nfill, lambda tl: tile_copy_out(row0, tl).wait())


ITEM_TILES = 4
RING_TILES = 2 * ITEM_TILES
GATHER_GROUP = 64
GROUPS_PER_TILE = MOE_TM // GATHER_GROUP


def _experts(row_tok, item_e, item_row0, item_nt, item_fill, n_items, blk_idx, hn2p, w_gate, w_up, w_down, r_alloc):
    ni = item_e.shape[0]
    nsub = 2
    assert GROUPS_PER_TILE == 2 * nsub
    hn = D_EXPERT // nsub
    rows = ITEM_TILES * MOE_TM
    nstep = 2 * nsub

    ns = ni * nstep


    def gu_map(kh):
        return lambda w, s, rt, ie, ir, nt, fl, n, bi: (bi[w * nstep + s], kh, bi[ns + w * nstep + s])

    def dl_map(kh):
        return lambda w, s, rt, ie, ir, nt, fl, n, bi: (bi[2 * ns + w * nstep + s], kh, bi[3 * ns + w * nstep + s])

    def dh_map(kh):
        return lambda w, s, rt, ie, ir, nt, fl, n, bi: (bi[2 * ns + w * nstep + s], kh,
                                                        nsub + bi[3 * ns + w * nstep + s])

    return pl.pallas_call(
        functools.partial(_experts_body, nsub=nsub),
        grid_spec=pltpu.PrefetchScalarGridSpec(
            num_scalar_prefetch=7,
            grid=(ni, nstep),
            in_specs=[pl.BlockSpec(memory_space=pl.ANY),
                      pl.BlockSpec((None, D_MODEL // 2, hn), gu_map(0)),
                      pl.BlockSpec((None, D_MODEL // 2, hn), gu_map(1)),
                      pl.BlockSpec((None, D_MODEL // 2, hn), gu_map(0)),
                      pl.BlockSpec((None, D_MODEL // 2, hn), gu_map(1)),
                      pl.BlockSpec((None, D_EXPERT // 2, hn), dl_map(0)),
                      pl.BlockSpec((None, D_EXPERT // 2, hn), dl_map(1)),
                      pl.BlockSpec((None, D_EXPERT // 2, hn), dh_map(0)),
                      pl.BlockSpec((None, D_EXPERT // 2, hn), dh_map(1))],
            out_specs=pl.BlockSpec(memory_space=pl.ANY),
            scratch_shapes=[pltpu.VMEM((rows, D_MODEL // 2), BF16),
                            pltpu.VMEM((rows, D_MODEL // 2), BF16),
                            pltpu.VMEM((rows, D_EXPERT), BF16),
                            pltpu.VMEM((rows, D_MODEL // 2), U32),
                            pltpu.VMEM((RING_TILES, MOE_TM, D_MODEL // 2), U32),
                            pltpu.SMEM((1,), I32),
                            pltpu.SemaphoreType.DMA((RING_TILES,)),
                            pltpu.SemaphoreType.DMA(())]),
        out_shape=jax.ShapeDtypeStruct((r_alloc, D_MODEL // 2), U32),
        compiler_params=_cparams(("arbitrary", "arbitrary")),
        name="experts",
    )(row_tok, item_e, item_row0, item_nt, item_fill, n_items, blk_idx, hn2p,
      w_gate, w_gate, w_up, w_up, w_down, w_down, w_down, w_down)


def _combine_body(d1_ref, d2_ref, h_ref, w_ref, y_ref, o_ref, buf, sem, *, tt, n_steps):
    i = pl.program_id(0)

    def copies(step, slot, r):
        t0 = step * tt + r
        c1 = pltpu.make_async_copy(y_ref.at[pl.ds(d1_ref[t0], 1)], buf.at[slot, 0, pl.ds(r, 1)], sem.at[slot])
        c2 = pltpu.make_async_copy(y_ref.at[pl.ds(d2_ref[t0], 1)], buf.at[slot, 1, pl.ds(r, 1)], sem.at[slot])
        return c1, c2

    def issue(step, slot):
        def body(r, c):
            c1, c2 = copies(step, slot, r)
            c1.start()
            c2.start()
            return c

        lax.fori_loop(0, tt, body, 0, unroll=8)

    @pl.when(i == 0)
    def _():
        issue(0, 0)

    @pl.when(i + 1 < n_steps)
    def _():
        issue(i + 1, (i + 1) % 2)

    slot = i % 2
    pltpu.make_async_copy(y_ref.at[pl.ds(0, tt)], buf.at[slot, 0], sem.at[slot]).wait()
    pltpu.make_async_copy(y_ref.at[pl.ds(0, tt)], buf.at[slot, 1], sem.at[slot]).wait()
    a_lo, a_hi = _unpack_bf16_pair(buf[slot, 0])
    b_lo, b_hi = _unpack_bf16_pair(buf[slot, 1])
    w1 = w_ref[:, 0:1]
    w2 = w_ref[:, 1:2]
    half = D_MODEL // 2
    o_ref[:, 0:half] = h_ref[:, 0:half] + (a_lo * w1 + b_lo * w2)
    o_ref[:, half:D_MODEL] = h_ref[:, half:D_MODEL] + (a_hi * w1 + b_hi * w2)


def _combine(d1, d2, h1, w_tok, y_rows):
    t = h1.shape[0]
    tt = 256
    n_steps = t // tt
    return pl.pallas_call(
        functools.partial(_combine_body, tt=tt, n_steps=n_steps),
        grid_spec=pltpu.PrefetchScalarGridSpec(
            num_scalar_prefetch=2,
            grid=(n_steps,),
            in_specs=[pl.BlockSpec((tt, D_MODEL), lambda i, d1, d2: (i, 0)),
                      pl.BlockSpec((tt, LANES), lambda i, d1, d2: (i, 0)),
                      pl.BlockSpec(memory_space=pl.ANY)],
            out_specs=pl.BlockSpec((tt, D_MODEL), lambda i, d1, d2: (i, 0)),
            scratch_shapes=[pltpu.VMEM((2, 2, tt, D_MODEL // 2), U32),
                            pltpu.SemaphoreType.DMA((2,))]),
        out_shape=jax.ShapeDtypeStruct((t, D_MODEL), F32),
        compiler_params=_cparams(("arbitrary",)),
        name="moe_combine",
    )(d1, d2, h1, w_tok, y_rows)


def _band_bias():
    qi = np.arange(ATTN_SPAN)[:, None]
    kj = np.arange(ATTN_SPAN)[None, :]
    prev = np.where(kj >= qi, 0.0, NEG)
    own = np.where(kj <= qi, 0.0, NEG)
    with_prev = np.concatenate([prev, own], axis=1)
    no_prev = np.concatenate([np.full_like(prev, NEG), own], axis=1)
    both = np.stack([np.tile(with_prev, (2, 1)), np.tile(no_prev, (2, 1))])
    return jnp.asarray(both, F32)


def kernel(x, positions, norm1_w, w_in, conv_w, conv_b, dt_bias, A_log, D_skip, ssd_norm_w, q_norm_w, k_norm_w,
           w_out, norm2_w, w_group_router, b_group_router, w_expert_router, b_expert_router, w_gate, w_up, w_down):
    b, s, _ = x.shape
    assert b == 1 and norm1_w.shape[0] == 1
    t = s
    x2 = x.reshape(t, D_MODEL)

    w_in0 = w_in[0]
    zc, xc_, bc_, cc_, dtc, qc, kc, vc = np.cumsum((0, 2048, 2048, 512, 512, 32, 1024, 1024))
    w_main = jnp.concatenate([w_in0[:, :dtc].astype(BF16), w_in0[:, qc:].astype(BF16)], axis=1)
    w_dt = jnp.pad(w_in0[:, dtc:qc], ((0, 0), (0, LANES - SSD_HEADS))).astype(BF16)
    pad_h = lambda v: jnp.pad(v.astype(F32), (0, LANES - SSD_HEADS)).reshape(1, LANES)
    lane = np.arange(LANES)
    c64 = lane % ATTN_HEAD_DIM
    inv_freq = 1.0 / (ROPE_THETA ** (jnp.arange(0, ROPE_DIM, 2, dtype=F32) / ROPE_DIM))
    f_lane = jnp.where(jnp.asarray(c64 < ROPE_DIM), inv_freq[jnp.asarray(c64 % (ROPE_DIM // 2))], 0.0).reshape(1, LANES)
    sg_lane = jnp.asarray(np.where(c64 < ROPE_DIM // 2, -1.0, np.where(c64 < ROPE_DIM, 1.0, 0.0)), F32).reshape(1, LANES)
    hsum = jnp.asarray((lane[:, None] // ATTN_HEAD_DIM) == (lane[None, :] // ATTN_HEAD_DIM), BF16)
    hexp = jnp.asarray(np.arange(LANES)[:, None] == (np.arange(SSD_D_INNER)[None, :] // SSD_HEAD_DIM), BF16)
    qw_lane = jnp.tile(q_norm_w[0].astype(F32), 2).reshape(1, LANES)
    kw_lane = jnp.tile(k_norm_w[0].astype(F32), 2).reshape(1, LANES)
    dsk = jnp.repeat(D_skip[0].astype(F32), SSD_HEAD_DIM).reshape(1, SSD_D_INNER)
    wr = jnp.concatenate([w_group_router[0],
                          jnp.transpose(w_expert_router[0], (1, 0, 2)).reshape(D_MODEL, N_EXPERTS)], axis=1)
    wr_f = jnp.pad(wr.T.astype(F32), ((0, LANES - N_GROUPS - N_EXPERTS), (0, 0)))
    wr_hi = wr_f.astype(BF16)
    wr_t = jnp.stack([wr_hi, (wr_f - wr_hi.astype(F32)).astype(BF16)])
    br = jnp.pad(jnp.concatenate([b_group_router[0], b_expert_router[0].reshape(-1)]).astype(F32),
                 (0, LANES - N_GROUPS - N_EXPERTS)).reshape(LANES, 1)

    cos_t, sin_t = _rope_table(positions.reshape(t, 1), f_lane, sg_lane)
    proj, dtraw = _inproj(x2, norm1_w.astype(F32), w_main, w_dt)
    y_ssd = _ssd(proj, dtraw, conv_w[0].astype(F32), conv_b.astype(F32), pad_h(dt_bias[0]), pad_h(A_log[0]),
                 dsk, ssd_norm_w.astype(F32), hexp)
    y_attn = _attention(proj, cos_t, sin_t, qw_lane, kw_lane, hsum, _band_bias())
    h1, hn2p, lg_t = _outproj(y_ssd, y_attn, x2, w_out[0].astype(BF16), norm2_w.astype(F32), wr_t)

    rt_tt = 512
    utri = jnp.asarray(np.arange(rt_tt)[:, None] <= np.arange(rt_tt)[None, :], BF16)
    oi, of, cnt = _route(lg_t, br, utri)
    counts = cnt[:, 0].astype(I32)
    n_tiles_max = (2 * t + N_EXPERTS * (MOE_TM - 1) + MOE_TM - 1) // MOE_TM
    r_alloc = n_tiles_max * MOE_TM
    tiles_e = (counts + MOE_TM - 1) // MOE_TM
    tile_end = jnp.cumsum(tiles_e)
    n_used = tile_end[-1]
    poff = ((tile_end - tiles_e) * MOE_TM).reshape(N_EXPERTS, 1)
    nt = n_used.reshape(1).astype(I32)
    dst = _dest(oi, poff.astype(I32))
    row_tok = _rowtok(dst[0], dst[1], r_alloc + ITEM_TILES * MOE_TM)

    items_e = (tiles_e + ITEM_TILES - 1) // ITEM_TILES
    item_end = jnp.cumsum(items_e)
    n_items = item_end[-1]
    ni_max = (n_tiles_max + (ITEM_TILES - 1) * N_EXPERTS + ITEM_TILES - 1) // ITEM_TILES
    w_idx = jnp.arange(ni_max, dtype=I32)
    w_live = w_idx < n_items
    w_c = jnp.minimum(w_idx, n_items - 1)
    item_e = jnp.searchsorted(item_end, w_c, side="right").astype(I32)
    k_in_e = w_c - (item_end - items_e)[item_e]
    spare = w_idx - n_items
    item_row0 = jnp.where(w_live, poff[item_e, 0] + k_in_e * (ITEM_TILES * MOE_TM),
                          (n_used + ITEM_TILES * spare) * MOE_TM).astype(I32)
    item_nt = jnp.where(w_live, jnp.clip(tiles_e[item_e] - ITEM_TILES * k_in_e, 0, ITEM_TILES), 0).astype(I32)
    item_fill = jnp.where(w_live, 0,
                          jnp.clip(n_tiles_max - n_used - ITEM_TILES * spare, 0, ITEM_TILES)).astype(I32)

    e_last = item_e[jnp.maximum(n_items - 1, 0)]
    e_next = jnp.where(w_idx + 1 < n_items, item_e[jnp.minimum(w_idx + 1, ni_max - 1)], e_last)
    e_prev = item_e[jnp.maximum(w_idx - 1, 0)]
    gu_e = jnp.stack([item_e, item_e, e_next, e_next], axis=1)
    gu_h = jnp.broadcast_to(jnp.asarray([0, 1, 0, 0], I32), (ni_max, 4))
    dn_e = jnp.stack([e_prev, e_prev, item_e, item_e], axis=1)
    dn_h = jnp.where((w_idx == 0)[:, None], jnp.asarray([0, 0, 0, 1], I32), jnp.asarray([1, 1, 0, 1], I32))
    dead = jnp.logical_not(w_live)[:, None]
    gu_e = jnp.where(dead, e_last, gu_e)
    gu_h = jnp.where(dead, 0, gu_h)
    dn_e = jnp.where(dead, e_last, dn_e)
    dn_h = jnp.where(dead, 1, dn_h)
    blk_idx = jnp.concatenate([gu_e.reshape(-1), gu_h.reshape(-1), dn_e.reshape(-1), dn_h.reshape(-1)]).astype(I32)

    y_rows = _experts(row_tok, item_e, item_row0, item_nt, item_fill, n_items.reshape(1).astype(I32), blk_idx,
                      hn2p, w_gate[0], w_up[0], w_down[0], r_alloc)
    w_tok = jnp.pad(of[0:2].T, ((0, 0), (0, LANES - 2)))
    out = _combine(dst[0], dst[1], h1, w_tok, y_rows)
    return out.reshape(b, s, D_MODEL)
```

```python
import functools

import jax
import jax.numpy as jnp
import numpy as np
from jax import lax
from jax.experimental import pallas as pl
from jax.experimental.pallas import tpu as pltpu

F32 = jnp.float32
BF16 = jnp.bfloat16
I32 = jnp.int32
U32 = jnp.uint32

D_MODEL = 2048
SSD_HEADS = 32
SSD_HEAD_DIM = 64
SSD_D_INNER = 2048
SSD_GROUPS = 4
SSD_D_STATE = 128
SSD_CONV = 4
SSD_BC = SSD_GROUPS * SSD_D_STATE
SSD_CONV_DIM = SSD_D_INNER + 2 * SSD_BC
ATTN_HEADS = 16
ATTN_HEAD_DIM = 64
ATTN_D = 1024
ATTN_SPAN = 128
DILATIONS = (1, 4, 16)
ROPE_DIM = 16
ROPE_THETA = 500000.0
N_GROUPS = 8
EXPERTS_PER_GROUP = 8
N_EXPERTS = 64
D_EXPERT = 1024
EPS = 1e-6
NEG = -1e30

LANES = 128
VMEM_LIMIT = 56 * 1024 * 1024

COL_Z, COL_XS, COL_B, COL_C, COL_Q, COL_K, COL_V = 0, 2048, 4096, 4608, 5120, 6144, 7168
PROJ_COLS = 8192

SSD_L = 128
ATT_BLK = 2048
MOE_TM = 256


def _cparams(sem):
    return pltpu.CompilerParams(dimension_semantics=sem, vmem_limit_bytes=VMEM_LIMIT)


def _silu(v):
    return v * (1.0 / (1.0 + jnp.exp(-v)))


def _rope_body(pos_ref, f_ref, sg_ref, cos_ref, sin_ref):
    ang = pos_ref[...].astype(F32) * f_ref[...]
    cos_ref[...] = jnp.cos(ang)
    sin_ref[...] = jnp.sin(ang) * sg_ref[...]


def _rope_table(pos_col, f_lane, sg_lane):
    t = pos_col.shape[0]
    tt = min(t, 2048)
    return pl.pallas_call(
        _rope_body,
        grid=(t // tt,),
        in_specs=[pl.BlockSpec((tt, 1), lambda i: (i, 0)),
                  pl.BlockSpec((1, LANES), lambda i: (0, 0)),
                  pl.BlockSpec((1, LANES), lambda i: (0, 0))],
        out_specs=[pl.BlockSpec((tt, LANES), lambda i: (i, 0))] * 2,
        out_shape=[jax.ShapeDtypeStruct((t, LANES), F32)] * 2,
        compiler_params=_cparams(("arbitrary",)),
        name="rope_table",
    )(pos_col, f_lane, sg_lane)


def _inproj_body(x_ref, nw_ref, w_ref, wdt_ref, proj_ref, dt_ref, hn_ref, *, tm, rc):
    j = pl.program_id(1)

    @pl.when(j == 0)
    def _():
        def chunk(c, carry):
            r = pl.multiple_of(c * rc, rc)
            xf = x_ref[pl.ds(r, rc), :]
            ms = jnp.mean(xf * xf, axis=-1, keepdims=True)
            hn_ref[pl.ds(r, rc), :] = (xf * lax.rsqrt(ms + EPS) * nw_ref[...]).astype(BF16)
            return carry

        lax.fori_loop(0, tm // rc, chunk, 0)
        dt_ref[...] = jnp.dot(hn_ref[...], wdt_ref[...], preferred_element_type=F32)

    proj_ref[...] = jnp.dot(hn_ref[...], w_ref[...], preferred_element_type=F32).astype(BF16)


def _inproj(x2, nw, w, wdt):
    t = x2.shape[0]
    tm, tn = min(t, 1024), 1024
    return pl.pallas_call(
        functools.partial(_inproj_body, tm=tm, rc=128),
        grid=(t // tm, PROJ_COLS // tn),
        in_specs=[pl.BlockSpec((tm, D_MODEL), lambda i, j: (i, 0)),
                  pl.BlockSpec((1, D_MODEL), lambda i, j: (0, 0)),
                  pl.BlockSpec((D_MODEL, tn), lambda i, j: (0, j)),
                  pl.BlockSpec((D_MODEL, LANES), lambda i, j: (0, 0))],
        out_specs=[pl.BlockSpec((tm, tn), lambda i, j: (i, j)),
                   pl.BlockSpec((tm, LANES), lambda i, j: (i, 0))],
        out_shape=[jax.ShapeDtypeStruct((t, PROJ_COLS), BF16),
                   jax.ShapeDtypeStruct((t, LANES), F32)],
        scratch_shapes=[pltpu.VMEM((tm, D_MODEL), BF16)],
        compiler_params=_cparams(("arbitrary", "arbitrary")),
        name="inproj",
    )(x2, nw, w, wdt)


def _ssd_body(z_ref, xs_ref, b_ref, c_ref, dtr_ref, cw_ref, cb_ref, dtb_ref, alog_ref, dsk_ref, nw_ref,
              hexp_ref, shift_ref, y_ref, ubuf, xc, bc, cc, ybuf, s_ref, *, L):
    ci = pl.program_id(0)

    @pl.when(ci == 0)
    def _():
        ubuf[0:CONV_HALO, :] = jnp.zeros((CONV_HALO, SSD_CONV_DIM), BF16)
        s_ref[...] = jnp.zeros_like(s_ref)

    ubuf[CONV_HALO:CONV_HALO + L, 0:SSD_D_INNER] = xs_ref[...]
    ubuf[CONV_HALO:CONV_HALO + L, SSD_D_INNER:SSD_D_INNER + SSD_BC] = b_ref[...]
    ubuf[CONV_HALO:CONV_HALO + L, SSD_D_INNER + SSD_BC:SSD_CONV_DIM] = c_ref[...]
    cw = 512
    for cch in range(SSD_CONV_DIM // cw):
        cs_ = slice(cw * cch, cw * cch + cw)
        taps = jnp.dot(shift_ref[...], ubuf[:, cs_], preferred_element_type=F32)
        acc = cb_ref[:, cs_] + cw_ref[0:1, cs_] * taps[0:L]
        for k in range(1, SSD_CONV):
            acc = acc + cw_ref[k:k + 1, cs_] * taps[k * L:(k + 1) * L]
        act = _silu(acc)
        if cch < 4:
            xc[:, cs_] = act
        elif cch == 4:
            bc[...] = act
        else:
            cc[...] = act
    ubuf[0:CONV_HALO, :] = ubuf[L:L + CONV_HALO, :]

    dt_in = dtr_ref[...] + dtb_ref[...]
    dt = jnp.maximum(dt_in, 0.0) + jnp.log1p(jnp.exp(-jnp.abs(dt_in)))
    a = dt * (-jnp.exp(alog_ref[...]))
    row = lax.broadcasted_iota(I32, (L, L), 0)
    col = lax.broadcasted_iota(I32, (L, L), 1)
    causal = col <= row
    cs = jnp.dot(causal.astype(F32), a, precision=lax.Precision.HIGHEST, preferred_element_type=F32)
    cs_last = cs[L - 1:L, :]
    wmat = jnp.exp(cs_last - cs) * dt
    cs_t = cs.T
    dt_t = dt.T
    cdec = jnp.broadcast_to(jnp.exp(cs_last), (16, LANES))
    c1 = cdec.astype(BF16)
    r1 = cdec - c1.astype(F32)
    c2 = r1.astype(BF16)
    c3 = (r1 - c2.astype(F32)).astype(BF16)
    cdec_x = (jnp.dot(c1, hexp_ref[...], preferred_element_type=F32)
              + jnp.dot(c2, hexp_ref[...], preferred_element_type=F32)
              + jnp.dot(c3, hexp_ref[...], preferred_element_type=F32))[0:1, :]
    lane = lax.broadcasted_iota(I32, (L, LANES), 1)
    first = lane < SSD_HEAD_DIM

    for g in range(SSD_GROUPS):
        bg = bc[:, LANES * g:LANES * g + LANES]
        cg = cc[:, LANES * g:LANES * g + LANES]
        cb = lax.dot_general(cg.astype(BF16), bg.astype(BF16), (((1,), (1,)), ((), ())),
                             preferred_element_type=F32)
        bg_t = bg.T.astype(BF16)
        xw_parts = []
        for q in range(4):
            lanes_ = slice(512 * g + LANES * q, 512 * g + LANES * q + LANES)
            x_pair = xc[:, lanes_]
            s_pair = s_ref[g, :, LANES * q:LANES * q + LANES]
            rhs = jnp.concatenate([x_pair.astype(BF16), s_pair.astype(BF16)], axis=0)
            ys, wbs = [], []
            for e2 in range(2):
                h = 8 * g + 2 * q + e2
                cs_col = jnp.broadcast_to(cs[:, h:h + 1], (L, L))
                lm = jnp.exp(jnp.where(causal, cs_col - cs_t[h:h + 1, :], NEG))
                m = cb * lm * dt_t[h:h + 1, :]
                e_col = jnp.exp(jnp.broadcast_to(cs[:, h:h + 1], (L, LANES)))
                lhs = jnp.concatenate([m.astype(BF16), (cg * e_col).astype(BF16)], axis=1)
                ys.append(jnp.dot(lhs, rhs, preferred_element_type=F32))
                wbs.append(jnp.broadcast_to(wmat[:, h:h + 1], (L, LANES)))
            ybuf[:, lanes_] = jnp.where(first, ys[0], ys[1])
            xw_parts.append((x_pair * jnp.where(first, wbs[0], wbs[1])).astype(BF16))
        xw_g = jnp.concatenate(xw_parts, axis=1)
        s_ref[g] = (s_ref[g] * cdec_x[:, 512 * g:512 * g + 512]
                    + jnp.dot(bg_t, xw_g, preferred_element_type=F32))

    for g in range(SSD_GROUPS):
        gs = slice(512 * g, 512 * g + 512)
        zf = z_ref[:, gs].astype(F32)
        yg = (ybuf[:, gs] + dsk_ref[:, gs] * xc[:, gs]) * _silu(zf)
        ms = jnp.mean(yg * yg, axis=-1, keepdims=True)
        y_ref[:, gs] = (yg * lax.rsqrt(ms + EPS) * nw_ref[:, gs]).astype(BF16)


CONV_HALO = 16


def _conv_shift_matrix(L):
    s = np.zeros((SSD_CONV * L, CONV_HALO + L), np.float32)
    for k in range(SSD_CONV):
        s[k * L + np.arange(L), CONV_HALO + np.arange(L) - (SSD_CONV - 1) + k] = 1.0
    return jnp.asarray(s, BF16)


def _ssd(proj, dtraw, cw, cb, dtb, alog, dsk, nw, hexp):
    t = proj.shape[0]
    L = SSD_L
    shift = _conv_shift_matrix(L)
    full = lambda shape: pl.BlockSpec(shape, lambda i: (0,) * len(shape))
    return pl.pallas_call(
        functools.partial(_ssd_body, L=L),
        grid=(t // L,),
        in_specs=[pl.BlockSpec((L, SSD_D_INNER), lambda i: (i, COL_Z // SSD_D_INNER)),
                  pl.BlockSpec((L, SSD_D_INNER), lambda i: (i, COL_XS // SSD_D_INNER)),
                  pl.BlockSpec((L, SSD_BC), lambda i: (i, COL_B // SSD_BC)),
                  pl.BlockSpec((L, SSD_BC), lambda i: (i, COL_C // SSD_BC)),
                  pl.BlockSpec((L, LANES), lambda i: (i, 0)),
                  full((SSD_CONV, SSD_CONV_DIM)), full((1, SSD_CONV_DIM)),
                  full((1, LANES)), full((1, LANES)),
                  full((1, SSD_D_INNER)), full((1, SSD_D_INNER)), full((LANES, SSD_D_INNER)),
                  full((SSD_CONV * L, CONV_HALO + L))],
        out_specs=pl.BlockSpec((L, SSD_D_INNER), lambda i: (i, 0)),
        out_shape=jax.ShapeDtypeStruct((t, SSD_D_INNER), BF16),
        scratch_shapes=[pltpu.VMEM((CONV_HALO + L, SSD_CONV_DIM), BF16),
                        pltpu.VMEM((L, SSD_D_INNER), F32),
                        pltpu.VMEM((L, SSD_BC), F32),
                        pltpu.VMEM((L, SSD_BC), F32),
                        pltpu.VMEM((L, SSD_D_INNER), F32),
                        pltpu.VMEM((SSD_GROUPS, SSD_D_STATE, 512), F32)],
        compiler_params=_cparams(("arbitrary",)),
        name="ssd_scan",
    )(proj, proj, proj, proj, dtraw, cw, cb, dtb, alog, dsk, nw, hexp, shift)


def _attn_body(q_ref, k_ref, v_ref, cos_ref, sin_ref, qw_ref, kw_ref, hsum_ref, bias_ref, o_ref,
               q_s, k_s, v_s, o_b, m_b, l_b, *, blk):
    j = pl.program_id(1)
    cur = (j % 2) * blk
    lane = lax.broadcasted_iota(I32, (1, LANES), 1)
    c64 = lane % ATTN_HEAD_DIM
    low = c64 < (ROPE_DIM // 2)

    def norm_rope(raw, w_lane):
        xf = raw.astype(F32)
        x2 = xf * xf
        hi = x2.astype(BF16)
        lo = (x2 - hi.astype(F32)).astype(BF16)
        ss = (jnp.dot(hi, hsum_ref[...], preferred_element_type=F32)
              + jnp.dot(lo, hsum_ref[...], preferred_element_type=F32))
        y = xf * lax.rsqrt(ss * (1.0 / ATTN_HEAD_DIM) + EPS) * w_lane
        partner = jnp.where(low, pltpu.roll(y, LANES - ROPE_DIM // 2, 1), pltpu.roll(y, ROPE_DIM // 2, 1))
        return y * cos_ref[...] + partner * sin_ref[...]

    @pl.when(j == 0)
    def _():
        k_s[pl.ds(blk, blk), :] = jnp.zeros((blk, LANES), F32)
        v_s[pl.ds(blk, blk), :] = jnp.zeros((blk, LANES), F32)

    q_s[...] = norm_rope(q_ref[...], qw_ref[...]) * (ATTN_HEAD_DIM ** -0.5)
    k_s[pl.ds(cur, blk), :] = norm_rope(k_ref[...], kw_ref[...])
    v_s[pl.ds(cur, blk), :] = v_ref[...].astype(F32)
    lane_q = lax.broadcasted_iota(I32, (ATTN_SPAN, LANES), 1)
    first = lane_q < ATTN_HEAD_DIM
    mask0 = first.astype(F32)
    mask1 = 1.0 - mask0
    ones_v = jnp.ones((2 * ATTN_SPAN, LANES), BF16)

    def rows(ref, start, d):
        if d == 1:
            return ref[pl.ds(start, ATTN_SPAN), :]
        return ref[pl.ds(start, ATTN_SPAN, stride=d), :]

    def tile(br, start, d):
        qv = rows(q_s, start, d)
        qs = jnp.concatenate([qv * mask0, qv * mask1], axis=0).astype(BF16)
        prev_start = (cur + start - ATTN_SPAN * d) & (2 * blk - 1)
        own_start = cur + start
        kt = jnp.concatenate([rows(k_s, prev_start, d), rows(k_s, own_start, d)], axis=0).astype(BF16)
        vt = jnp.concatenate([rows(v_s, prev_start, d), rows(v_s, own_start, d)], axis=0).astype(BF16)
        s = lax.dot_general(qs, kt, (((1,), (1,)), ((), ())), preferred_element_type=F32)
        no_prev = jnp.logical_and(j == 0, start < ATTN_SPAN * d)
        s = s + bias_ref[no_prev.astype(I32)]
        m2 = jnp.max(s, axis=-1, keepdims=True)
        p = jnp.exp(s - m2).astype(BF16)
        o2 = jnp.dot(p, jnp.concatenate([vt, ones_v], axis=1), preferred_element_type=F32)
        m_rep = jnp.broadcast_to(m2, (2 * ATTN_SPAN, LANES))
        if d == 1:
            sl = pl.ds(start, ATTN_SPAN)
        else:
            sl = pl.ds(start, ATTN_SPAN, stride=d)
        o_b[br, sl, :] = jnp.where(first, o2[:ATTN_SPAN, :LANES], o2[ATTN_SPAN:, :LANES])
        l_b[br, sl, :] = jnp.where(first, o2[:ATTN_SPAN, LANES:], o2[ATTN_SPAN:, LANES:])
        m_b[br, sl, :] = jnp.where(first, m_rep[:ATTN_SPAN], m_rep[ATTN_SPAN:])

    for br, d in enumerate(DILATIONS):
        n_res = d
        n_sub = blk // (d * ATTN_SPAN)

        def loop_body(i, carry, br=br, d=d, n_res=n_res):
            r = i % n_res
            sub = i // n_res
            tile(br, r + sub * (d * ATTN_SPAN), d)
            return carry

        lax.fori_loop(0, n_res * n_sub, loop_body, 0, unroll=8)

    def merge(c, carry):
        rs = pl.ds(pl.multiple_of(c * ATTN_SPAN, ATTN_SPAN), ATTN_SPAN)
        ms = [m_b[br, rs, :] for br in range(len(DILATIONS))]
        m_all = jnp.maximum(jnp.maximum(ms[0], ms[1]), ms[2])
        ws = [jnp.exp(m - m_all) for m in ms]
        num = ws[0] * o_b[0, rs, :] + ws[1] * o_b[1, rs, :] + ws[2] * o_b[2, rs, :]
        den = ws[0] * l_b[0, rs, :] + ws[1] * l_b[1, rs, :] + ws[2] * l_b[2, rs, :]
        o_ref[rs, :] = (num / den).astype(BF16)
        return carry

    lax.fori_loop(0, blk // ATTN_SPAN, merge, 0)


def _attention(proj, cos_t, sin_t, qw_lane, kw_lane, hsum, bias):
    t = proj.shape[0]
    blk = ATT_BLK
    n_hp = ATTN_HEADS // 2
    colblk = lambda base: (lambda hp, j: (j, base // LANES + hp))
    full = lambda shape: pl.BlockSpec(shape, lambda hp, j: (0,) * len(shape))
    return pl.pallas_call(
        functools.partial(_attn_body, blk=blk),
        grid=(n_hp, t // blk),
        in_specs=[pl.BlockSpec((blk, LANES), colblk(COL_Q)),
                  pl.BlockSpec((blk, LANES), colblk(COL_K)),
                  pl.BlockSpec((blk, LANES), colblk(COL_V)),
                  pl.BlockSpec((blk, LANES), lambda hp, j: (j, 0)),
                  pl.BlockSpec((blk, LANES), lambda hp, j: (j, 0)),
                  full((1, LANES)), full((1, LANES)), full((LANES, LANES)),
                  full((2, 2 * ATTN_SPAN, 2 * ATTN_SPAN))],
        out_specs=pl.BlockSpec((blk, LANES), lambda hp, j: (j, hp)),
        out_shape=jax.ShapeDtypeStruct((t, ATTN_D), BF16),
        scratch_shapes=[pltpu.VMEM((blk, LANES), F32),
                        pltpu.VMEM((2 * blk, LANES), F32),
                        pltpu.VMEM((2 * blk, LANES), F32),
                        pltpu.VMEM((len(DILATIONS), blk, LANES), F32),
                        pltpu.VMEM((len(DILATIONS), blk, LANES), F32),
                        pltpu.VMEM((len(DILATIONS), blk, LANES), F32)],
        compiler_params=_cparams(("arbitrary", "arbitrary")),
        name="dilated_attn",
    )(proj, proj, proj, cos_t, sin_t, qw_lane, kw_lane, hsum, bias)


def _pack_bf16_pair(lo_f32, hi_f32):
    lo_bits = pltpu.bitcast(lo_f32.astype(BF16).astype(F32), U32)
    hi_bits = pltpu.bitcast(hi_f32.astype(BF16).astype(F32), U32)
    return (lo_bits >> 16) | (hi_bits & jnp.uint32(0xFFFF0000))


def _unpack_bf16_pair(w):
    lo = pltpu.bitcast(w << 16, F32)
    hi = pltpu.bitcast(w & jnp.uint32(0xFFFF0000), F32)
    return lo, hi


def _outproj_body(ys_ref, ya_ref, x_ref, w_ref, nw_ref, wr_ref, h_ref, hn_ref, lg_ref):
    acc = jnp.dot(ys_ref[...], w_ref[0:SSD_D_INNER, :], preferred_element_type=F32)
    acc = acc + jnp.dot(ya_ref[...], w_ref[SSD_D_INNER:SSD_D_INNER + ATTN_D, :], preferred_element_type=F32)
    h = x_ref[...] + acc
    h_ref[...] = h
    ms = jnp.mean(h * h, axis=-1, keepdims=True)
    hn = h * lax.rsqrt(ms + EPS) * nw_ref[...]
    half = D_MODEL // 2
    hn_ref[...] = _pack_bf16_pair(hn[:, :half], hn[:, half:])
    hn_hi = hn.astype(BF16)
    hn_lo = (hn - hn_hi.astype(F32)).astype(BF16)
    nt_dims = (((1,), (1,)), ((), ()))
    lg_ref[...] = (lax.dot_general(wr_ref[0], hn_hi, nt_dims, preferred_element_type=F32)
                   + lax.dot_general(wr_ref[0], hn_lo, nt_dims, preferred_element_type=F32)
                   + lax.dot_general(wr_ref[1], hn_hi, nt_dims, preferred_element_type=F32))


def _outproj(yssd, yattn, x2, w, nw, wr_t):
    t = x2.shape[0]
    tm = 512
    full = lambda shape: pl.BlockSpec(shape, lambda i: (0,) * len(shape))
    return pl.pallas_call(
        _outproj_body,
        grid=(t // tm,),
        in_specs=[pl.BlockSpec((tm, SSD_D_INNER), lambda i: (i, 0)),
                  pl.BlockSpec((tm, ATTN_D), lambda i: (i, 0)),
                  pl.BlockSpec((tm, D_MODEL), lambda i: (i, 0)),
                  full((SSD_D_INNER + ATTN_D, D_MODEL)), full((1, D_MODEL)), full((2, LANES, D_MODEL))],
        out_specs=[pl.BlockSpec((tm, D_MODEL), lambda i: (i, 0)),
                   pl.BlockSpec((tm, D_MODEL // 2), lambda i: (i, 0)),
                   pl.BlockSpec((LANES, tm), lambda i: (0, i))],
        out_shape=[jax.ShapeDtypeStruct((t, D_MODEL), F32),
                   jax.ShapeDtypeStruct((t, D_MODEL // 2), U32),
                   jax.ShapeDtypeStruct((LANES, t), F32)],
        compiler_params=_cparams(("arbitrary",)),
        name="outproj",
    )(yssd, yattn, x2, w, nw, wr_t)


def _route_body(lg_ref, bias_ref, u_ref, oi_ref, of_ref, cnt_ref, carry, *, tt):
    i = pl.program_id(0)

    @pl.when(i == 0)
    def _():
        carry[...] = jnp.zeros_like(carry)

    lg = lg_ref[...] + bias_ref[...]
    sub8 = lax.broadcasted_iota(I32, (8, tt), 0)
    g = lg[0:8]
    gmax = jnp.max(g, axis=0, keepdims=True)
    gidx = jnp.min(jnp.where(g == gmax, sub8, 8), axis=0, keepdims=True)
    gval = 1.0 / jnp.sum(jnp.exp(g - gmax), axis=0, keepdims=True)
    esel = jnp.zeros((8, tt), F32)
    for grp in range(N_GROUPS):
        esel = jnp.where(gidx == grp, lg[8 + 8 * grp:16 + 8 * grp], esel)
    m1 = jnp.max(esel, axis=0, keepdims=True)
    i1 = jnp.min(jnp.where(esel == m1, sub8, 8), axis=0, keepdims=True)
    em = jnp.where(sub8 == i1, -jnp.inf, esel)
    m2 = jnp.max(em, axis=0, keepdims=True)
    i2 = jnp.min(jnp.where(em == m2, sub8, 8), axis=0, keepdims=True)
    r = jnp.exp(m2 - m1)
    w1 = gval * (1.0 / (1.0 + r))
    w2 = gval * (r / (1.0 + r))
    e1 = gidx * EXPERTS_PER_GROUP + i1
    e2 = gidx * EXPERTS_PER_GROUP + i2
    sub64 = lax.broadcasted_iota(I32, (N_EXPERTS, tt), 0)
    oh1 = sub64 == e1
    oh2 = sub64 == e2
    oh = oh1.astype(F32) + oh2.astype(F32)
    pref = jnp.dot(oh.astype(BF16), u_ref[...], preferred_element_type=F32)
    excl = carry[...] + pref - 1.0
    rank1 = jnp.sum(jnp.where(oh1, excl, 0.0), axis=0, keepdims=True)
    rank2 = jnp.sum(jnp.where(oh2, excl, 0.0), axis=0, keepdims=True)
    carry[...] = carry[...] + pref[:, tt - 1:tt]
    zi = jnp.zeros((4, tt), I32)
    oi_ref[...] = jnp.concatenate([e1, e2, rank1.astype(I32), rank2.astype(I32), zi], axis=0)
    of_ref[...] = jnp.concatenate([w1, w2, jnp.zeros((6, tt), F32)], axis=0)
    cnt_ref[...] = jnp.broadcast_to(carry[...], (N_EXPERTS, LANES))


def _route(lg_t, bias_col, utri):
    t = lg_t.shape[1]
    tt = utri.shape[0]
    return pl.pallas_call(
        functools.partial(_route_body, tt=tt),
        grid=(t // tt,),
        in_specs=[pl.BlockSpec((LANES, tt), lambda i: (0, i)),
                  pl.BlockSpec((LANES, 1), lambda i: (0, 0)),
                  pl.BlockSpec((tt, tt), lambda i: (0, 0))],
        out_specs=[pl.BlockSpec((8, tt), lambda i: (0, i)),
                   pl.BlockSpec((8, tt), lambda i: (0, i)),
                   pl.BlockSpec((N_EXPERTS, LANES), lambda i: (0, 0))],
        out_shape=[jax.ShapeDtypeStruct((8, t), I32),
                   jax.ShapeDtypeStruct((8, t), F32),
                   jax.ShapeDtypeStruct((N_EXPERTS, LANES), F32)],
        scratch_shapes=[pltpu.VMEM((N_EXPERTS, 1), F32)],
        compiler_params=_cparams(("arbitrary",)),
        name="route",
    )(lg_t, bias_col, utri)


def _dest_body(oi_ref, poff_ref, d_ref, *, tt):
    sub64 = lax.broadcasted_iota(I32, (N_EXPERTS, tt), 0)
    oi = oi_ref[...]
    poff = poff_ref[...]
    d1 = jnp.sum(jnp.where(sub64 == oi[0:1], poff, 0), axis=0, keepdims=True) + oi[2:3]
    d2 = jnp.sum(jnp.where(sub64 == oi[1:2], poff, 0), axis=0, keepdims=True) + oi[3:4]
    d_ref[...] = jnp.concatenate([d1, d2, jnp.zeros((6, tt), I32)], axis=0)


def _dest(oi, poff_col):
    t = oi.shape[1]
    tt = min(t, 2048)
    return pl.pallas_call(
        functools.partial(_dest_body, tt=tt),
        grid=(t // tt,),
        in_specs=[pl.BlockSpec((8, tt), lambda i: (0, i)),
                  pl.BlockSpec((N_EXPERTS, 1), lambda i: (0, 0))],
        out_specs=pl.BlockSpec((8, tt), lambda i: (0, i)),
        out_shape=jax.ShapeDtypeStruct((8, t), I32),
        compiler_params=_cparams(("arbitrary",)),
        name="dest_rows",
    )(oi, poff_col)


def _rowtok_body(d1_ref, d2_ref, rt_ref, *, t, r_alloc):
    def clear(i, c):
        rt_ref[i] = 0
        return c

    lax.fori_loop(0, r_alloc, clear, 0, unroll=8)

    def put(i, c):
        rt_ref[d1_ref[i]] = i
        rt_ref[d2_ref[i]] = i
        return c

    lax.fori_loop(0, t, put, 0, unroll=8)


def _rowtok(d1, d2, r_alloc):
    t = d1.shape[0]
    smem = pl.BlockSpec(memory_space=pltpu.SMEM)
    return pl.pallas_call(
        functools.partial(_rowtok_body, t=t, r_alloc=r_alloc),
        in_specs=[smem, smem],
        out_specs=smem,
        out_shape=jax.ShapeDtypeStruct((r_alloc,), I32),
        name="row_tokens",
    )(d1, d2)


def _experts_body(rt_ref, ie_ref, ir_ref, int_ref, ifl_ref, ni_ref, bi_ref, x_hbm,
                  wga_ref, wgb_ref, wua_ref, wub_ref, wdla_ref, wdlb_ref, wdha_ref, wdhb_ref,
                  y_hbm, xlo, xhi, hid, ybuf, ring, gp_ref, gsem, ysem, *, nsub):
    w = pl.program_id(0)
    s = pl.program_id(1)
    nt = int_ref[w]
    nfill = ifl_ref[w]
    row0 = ir_ref[w]
    tile0 = row0 // MOE_TM
    n_items = ni_ref[0]
    half = D_MODEL // 2
    hn = D_EXPERT // nsub

    def gather_group():
        g = gp_ref[0]
        base = g * GATHER_GROUP
        slot = (g // GROUPS_PER_TILE) % RING_TILES
        r0 = (g % GROUPS_PER_TILE) * GATHER_GROUP
        for j in range(GATHER_GROUP):
            tok = rt_ref[base + j]
            pltpu.make_async_copy(x_hbm.at[pl.ds(tok, 1)], ring.at[slot, pl.ds(r0 + j, 1)], gsem.at[slot]).start()
        gp_ref[0] = g + 1

    def ring_wait(tile):
        slot = tile % RING_TILES
        pltpu.make_async_copy(x_hbm.at[pl.ds(0, MOE_TM)], ring.at[slot], gsem.at[slot]).wait()

    def tile_copy_out(r0, tl):
        rs = pl.ds(pl.multiple_of(tl * MOE_TM, MOE_TM), MOE_TM)
        dst = pl.ds(pl.multiple_of(r0 + tl * MOE_TM, MOE_TM), MOE_TM)
        return pltpu.make_async_copy(ybuf.at[rs], y_hbm.at[dst], ysem)

    def for_tiles(n, fn):
        def body(tl, c):
            fn(tl)
            return c

        lax.fori_loop(0, n, body, 0)

    @pl.when(jnp.logical_and(w == 0, s == 0))
    def _():
        gp_ref[0] = 0

        def first(i, c):
            gather_group()
            return c

        lax.fori_loop(0, ITEM_TILES * GROUPS_PER_TILE, first, 0)

    @pl.when(jnp.logical_and(nt > 0, s == 0))
    def _():
        for_tiles(nt, lambda tl: ring_wait(tile0 + tl))

    def gateup(tl, step):
        rs = pl.ds(pl.multiple_of(tl * MOE_TM, MOE_TM), MOE_TM)
        if step == 0:
            lo, hi = _unpack_bf16_pair(ring[(tile0 + tl) % RING_TILES])
            a = lo.astype(BF16)
            b = hi.astype(BF16)
            xlo[rs, :] = a
            xhi[rs, :] = b
        else:
            gather_group()
            a = xlo[rs, :]
            b = xhi[rs, :]
        gate = (jnp.dot(a, wga_ref[...].astype(BF16), preferred_element_type=F32)
                + jnp.dot(b, wgb_ref[...].astype(BF16), preferred_element_type=F32))
        up = (jnp.dot(a, wua_ref[...].astype(BF16), preferred_element_type=F32)
              + jnp.dot(b, wub_ref[...].astype(BF16), preferred_element_type=F32))
        hid[rs, step * hn:(step + 1) * hn] = (_silu(gate) * up).astype(BF16)
        if step == 0:
            gather_group()

    def down(tl, step):
        rs = pl.ds(pl.multiple_of(tl * MOE_TM, MOE_TM), MOE_TM)
        gather_group()
        ha = hid[rs, 0:D_EXPERT // 2]
        hb = hid[rs, D_EXPERT // 2:D_EXPERT]
        ybuf[rs, step * hn:(step + 1) * hn] = _pack_bf16_pair(
            jnp.dot(ha, wdla_ref[...].astype(BF16), preferred_element_type=F32)
            + jnp.dot(hb, wdlb_ref[...].astype(BF16), preferred_element_type=F32),
            jnp.dot(ha, wdha_ref[...].astype(BF16), preferred_element_type=F32)
            + jnp.dot(hb, wdhb_ref[...].astype(BF16), preferred_element_type=F32))

    for step in range(nsub):
        @pl.when(jnp.logical_and(nt > 0, s == step))
        def _(step=step):
            for_tiles(nt, lambda tl: gateup(tl, step))

    @pl.when(jnp.logical_and(jnp.logical_and(nt > 0, w > 0), s == nsub))
    def _():
        for_tiles(int_ref[jnp.maximum(w - 1, 0)], lambda tl: tile_copy_out(row0, tl).wait())

    for step in range(nsub):
        @pl.when(jnp.logical_and(nt > 0, s == nsub + step))
        def _(step=step):
            for_tiles(nt, lambda tl: down(tl, step))

    @pl.when(jnp.logical_and(nt > 0, s == 2 * nsub - 1))
    def _():
        for_tiles(nt, lambda tl: tile_copy_out(row0, tl).start())

    @pl.when(jnp.logical_and(w == n_items - 1, s == 2 * nsub - 1))
    def _():
        for_tiles(nt, lambda tl: tile_copy_out(row0, tl).wait())
        for_tiles(ITEM_TILES, lambda tl: ring_wait(tile0 + nt + tl))

    @pl.when(jnp.logical_and(nfill > 0, s == 0))
    def _():
        ybuf[...] = jnp.zeros_like(ybuf)
        for_tiles(nfill, lambda tl: tile_copy_out(row0, tl).start())
        for_tiles(nfill, lambda tl: tile_copy_out(row0, tl).wait())


ITEM_TILES = 4
RING_TILES = 2 * ITEM_TILES
GATHER_GROUP = 64
GROUPS_PER_TILE = MOE_TM // GATHER_GROUP


def _experts(row_tok, item_e, item_row0, item_nt, item_fill, n_items, blk_idx, hn2p, w_gate, w_up, w_down, r_alloc):
    ni = item_e.shape[0]
    nsub = 2
    assert GROUPS_PER_TILE == 2 * nsub
    hn = D_EXPERT // nsub
    rows = ITEM_TILES * MOE_TM
    nstep = 2 * nsub

    ns = ni * nstep


    def gu_map(kh):
        return lambda w, s, rt, ie, ir, nt, fl, n, bi: (bi[w * nstep + s], kh, bi[ns + w * nstep + s])

    def dl_map(kh):
        return lambda w, s, rt, ie, ir, nt, fl, n, bi: (bi[2 * ns + w * nstep + s], kh, bi[3 * ns + w * nstep + s])

    def dh_map(kh):
        return lambda w, s, rt, ie, ir, nt, fl, n, bi: (bi[2 * ns + w * nstep + s], kh,
                                                        nsub + bi[3 * ns + w * nstep + s])

    return pl.pallas_call(
        functools.partial(_experts_body, nsub=nsub),
        grid_spec=pltpu.PrefetchScalarGridSpec(
            num_scalar_prefetch=7,
            grid=(ni, nstep),
            in_specs=[pl.BlockSpec(memory_space=pl.ANY),
                      pl.BlockSpec((None, D_MODEL // 2, hn), gu_map(0)),
                      pl.BlockSpec((None, D_MODEL // 2, hn), gu_map(1)),
                      pl.BlockSpec((None, D_MODEL // 2, hn), gu_map(0)),
                      pl.BlockSpec((None, D_MODEL // 2, hn), gu_map(1)),
                      pl.BlockSpec((None, D_EXPERT // 2, hn), dl_map(0)),
                      pl.BlockSpec((None, D_EXPERT // 2, hn), dl_map(1)),
                      pl.BlockSpec((None, D_EXPERT // 2, hn), dh_map(0)),
                      pl.BlockSpec((None, D_EXPERT // 2, hn), dh_map(1))],
            out_specs=pl.BlockSpec(memory_space=pl.ANY),
            scratch_shapes=[pltpu.VMEM((rows, D_MODEL // 2), BF16),
                            pltpu.VMEM((rows, D_MODEL // 2), BF16),
                            pltpu.VMEM((rows, D_EXPERT), BF16),
                            pltpu.VMEM((rows, D_MODEL // 2), U32),
                            pltpu.VMEM((RING_TILES, MOE_TM, D_MODEL // 2), U32),
                            pltpu.SMEM((1,), I32),
                            pltpu.SemaphoreType.DMA((RING_TILES,)),
                            pltpu.SemaphoreType.DMA(())]),
        out_shape=jax.ShapeDtypeStruct((r_alloc, D_MODEL // 2), U32),
        compiler_params=_cparams(("arbitrary", "arbitrary")),
        name="experts",
    )(row_tok, item_e, item_row0, item_nt, item_fill, n_items, blk_idx, hn2p,
      w_gate, w_gate, w_up, w_up, w_down, w_down, w_down, w_down)


def _combine_body(d1_ref, d2_ref, h_ref, w_ref, y_ref, o_ref, buf, sem, *, tt, n_steps):
    i = pl.program_id(0)

    def copies(step, slot, r):
        t0 = step * tt + r
        c1 = pltpu.make_async_copy(y_ref.at[pl.ds(d1_ref[t0], 1)], buf.at[slot, 0, pl.ds(r, 1)], sem.at[slot])
        c2 = pltpu.make_async_copy(y_ref.at[pl.ds(d2_ref[t0], 1)], buf.at[slot, 1, pl.ds(r, 1)], sem.at[slot])
        return c1, c2

    def issue(step, slot):
        def body(r, c):
            c1, c2 = copies(step, slot, r)
            c1.start()
            c2.start()
            return c

        lax.fori_loop(0, tt, body, 0, unroll=8)

    @pl.when(i == 0)
    def _():
        issue(0, 0)

    @pl.when(i + 1 < n_steps)
    def _():
        issue(i + 1, (i + 1) % 2)

    slot = i % 2
    pltpu.make_async_copy(y_ref.at[pl.ds(0, tt)], buf.at[slot, 0], sem.at[slot]).wait()
    pltpu.make_async_copy(y_ref.at[pl.ds(0, tt)], buf.at[slot, 1], sem.at[slot]).wait()
    a_lo, a_hi = _unpack_bf16_pair(buf[slot, 0])
    b_lo, b_hi = _unpack_bf16_pair(buf[slot, 1])
    w1 = w_ref[:, 0:1]
    w2 = w_ref[:, 1:2]
    half = D_MODEL // 2
    o_ref[:, 0:half] = h_ref[:, 0:half] + (a_lo * w1 + b_lo * w2)
    o_ref[:, half:D_MODEL] = h_ref[:, half:D_MODEL] + (a_hi * w1 + b_hi * w2)


def _combine(d1, d2, h1, w_tok, y_rows):
    t = h1.shape[0]
    tt = 256
    n_steps = t // tt
    return pl.pallas_call(
        functools.partial(_combine_body, tt=tt, n_steps=n_steps),
        grid_spec=pltpu.PrefetchScalarGridSpec(
            num_scalar_prefetch=2,
            grid=(n_steps,),
            in_specs=[pl.BlockSpec((tt, D_MODEL), lambda i, d1, d2: (i, 0)),
                      pl.BlockSpec((tt, LANES), lambda i, d1, d2: (i, 0)),
                      pl.BlockSpec(memory_space=pl.ANY)],
            out_specs=pl.BlockSpec((tt, D_MODEL), lambda i, d1, d2: (i, 0)),
            scratch_shapes=[pltpu.VMEM((2, 2, tt, D_MODEL // 2), U32),
                            pltpu.SemaphoreType.DMA((2,))]),
        out_shape=jax.ShapeDtypeStruct((t, D_MODEL), F32),
        compiler_params=_cparams(("arbitrary",)),
        name="moe_combine",
    )(d1, d2, h1, w_tok, y_rows)


def _band_bias():
    qi = np.arange(ATTN_SPAN)[:, None]
    kj = np.arange(ATTN_SPAN)[None, :]
    prev = np.where(kj >= qi, 0.0, NEG)
    own = np.where(kj <= qi, 0.0, NEG)
    with_prev = np.concatenate([prev, own], axis=1)
    no_prev = np.concatenate([np.full_like(prev, NEG), own], axis=1)
    both = np.stack([np.tile(with_prev, (2, 1)), np.tile(no_prev, (2, 1))])
    return jnp.asarray(both, F32)


def kernel(x, positions, norm1_w, w_in, conv_w, conv_b, dt_bias, A_log, D_skip, ssd_norm_w, q_norm_w, k_norm_w,
           w_out, norm2_w, w_group_router, b_group_router, w_expert_router, b_expert_router, w_gate, w_up, w_down):
    b, s, _ = x.shape
    assert b == 1 and norm1_w.shape[0] == 1
    t = s
    x2 = x.reshape(t, D_MODEL)

    w_in0 = w_in[0]
    zc, xc_, bc_, cc_, dtc, qc, kc, vc = np.cumsum((0, 2048, 2048, 512, 512, 32, 1024, 1024))
    w_main = jnp.concatenate([w_in0[:, :dtc].astype(BF16), w_in0[:, qc:].astype(BF16)], axis=1)
    w_dt = jnp.pad(w_in0[:, dtc:qc], ((0, 0), (0, LANES - SSD_HEADS))).astype(BF16)
    pad_h = lambda v: jnp.pad(v.astype(F32), (0, LANES - SSD_HEADS)).reshape(1, LANES)
    lane = np.arange(LANES)
    c64 = lane % ATTN_HEAD_DIM
    inv_freq = 1.0 / (ROPE_THETA ** (jnp.arange(0, ROPE_DIM, 2, dtype=F32) / ROPE_DIM))
    f_lane = jnp.where(jnp.asarray(c64 < ROPE_DIM), inv_freq[jnp.asarray(c64 % (ROPE_DIM // 2))], 0.0).reshape(1, LANES)
    sg_lane = jnp.asarray(np.where(c64 < ROPE_DIM // 2, -1.0, np.where(c64 < ROPE_DIM, 1.0, 0.0)), F32).reshape(1, LANES)
    hsum = jnp.asarray((lane[:, None] // ATTN_HEAD_DIM) == (lane[None, :] // ATTN_HEAD_DIM), BF16)
    hexp = jnp.asarray(np.arange(LANES)[:, None] == (np.arange(SSD_D_INNER)[None, :] // SSD_HEAD_DIM), BF16)
    qw_lane = jnp.tile(q_norm_w[0].astype(F32), 2).reshape(1, LANES)
    kw_lane = jnp.tile(k_norm_w[0].astype(F32), 2).reshape(1, LANES)
    dsk = jnp.repeat(D_skip[0].astype(F32), SSD_HEAD_DIM).reshape(1, SSD_D_INNER)
    wr = jnp.concatenate([w_group_router[0],
                          jnp.transpose(w_expert_router[0], (1, 0, 2)).reshape(D_MODEL, N_EXPERTS)], axis=1)
    wr_f = jnp.pad(wr.T.astype(F32), ((0, LANES - N_GROUPS - N_EXPERTS), (0, 0)))
    wr_hi = wr_f.astype(BF16)
    wr_t = jnp.stack([wr_hi, (wr_f - wr_hi.astype(F32)).astype(BF16)])
    br = jnp.pad(jnp.concatenate([b_group_router[0], b_expert_router[0].reshape(-1)]).astype(F32),
                 (0, LANES - N_GROUPS - N_EXPERTS)).reshape(LANES, 1)

    cos_t, sin_t = _rope_table(positions.reshape(t, 1), f_lane, sg_lane)
    proj, dtraw = _inproj(x2, norm1_w.astype(F32), w_main, w_dt)
    y_ssd = _ssd(proj, dtraw, conv_w[0].astype(F32), conv_b.astype(F32), pad_h(dt_bias[0]), pad_h(A_log[0]),
                 dsk, ssd_norm_w.astype(F32), hexp)
    y_attn = _attention(proj, cos_t, sin_t, qw_lane, kw_lane, hsum, _band_bias())
    h1, hn2p, lg_t = _outproj(y_ssd, y_attn, x2, w_out[0].astype(BF16), norm2_w.astype(F32), wr_t)

    rt_tt = 512
    utri = jnp.asarray(np.arange(rt_tt)[:, None] <= np.arange(rt_tt)[None, :], BF16)
    oi, of, cnt = _route(lg_t, br, utri)
    counts = cnt[:, 0].astype(I32)
    n_tiles_max = (2 * t + N_EXPERTS * (MOE_TM - 1) + MOE_TM - 1) // MOE_TM
    r_alloc = n_tiles_max * MOE_TM
    tiles_e = (counts + MOE_TM - 1) // MOE_TM
    tile_end = jnp.cumsum(tiles_e)
    n_used = tile_end[-1]
    poff = ((tile_end - tiles_e) * MOE_TM).reshape(N_EXPERTS, 1)
    nt = n_used.reshape(1).astype(I32)
    dst = _dest(oi, poff.astype(I32))
    row_tok = _rowtok(dst[0], dst[1], r_alloc + ITEM_TILES * MOE_TM)

    items_e = (tiles_e + ITEM_TILES - 1) // ITEM_TILES
    item_end = jnp.cumsum(items_e)
    n_items = item_end[-1]
    ni_max = (n_tiles_max + (ITEM_TILES - 1) * N_EXPERTS + ITEM_TILES - 1) // ITEM_TILES
    w_idx = jnp.arange(ni_max, dtype=I32)
    w_live = w_idx < n_items
    w_c = jnp.minimum(w_idx, n_items - 1)
    item_e = jnp.searchsorted(item_end, w_c, side="right").astype(I32)
    k_in_e = w_c - (item_end - items_e)[item_e]
    spare = w_idx - n_items
    item_row0 = jnp.where(w_live, poff[item_e, 0] + k_in_e * (ITEM_TILES * MOE_TM),
                          (n_used + ITEM_TILES * spare) * MOE_TM).astype(I32)
    item_nt = jnp.where(w_live, jnp.clip(tiles_e[item_e] - ITEM_TILES * k_in_e, 0, ITEM_TILES), 0).astype(I32)
    item_fill = jnp.where(w_live, 0,
                          jnp.clip(n_tiles_max - n_used - ITEM_TILES * spare, 0, ITEM_TILES)).astype(I32)

    e_last = item_e[jnp.maximum(n_items - 1, 0)]
    e_next = jnp.where(w_idx + 1 < n_items, item_e[jnp.minimum(w_idx + 1, ni_max - 1)], e_last)
    gu_e = jnp.stack([item_e, item_e, e_next, e_next], axis=1)
    gu_h = jnp.broadcast_to(jnp.asarray([0, 1, 0, 0], I32), (ni_max, 4))
    dn_e = jnp.stack([item_e, item_e, item_e, item_e], axis=1)
    dn_h = jnp.broadcast_to(jnp.asarray([0, 0, 0, 1], I32), (ni_max, 4))
    dead = jnp.logical_not(w_live)[:, None]
    gu_e = jnp.where(dead, e_last, gu_e)
    gu_h = jnp.where(dead, 0, gu_h)
    dn_e = jnp.where(dead, e_last, dn_e)
    dn_h = jnp.where(dead, 1, dn_h)
    blk_idx = jnp.concatenate([gu_e.reshape(-1), gu_h.reshape(-1), dn_e.reshape(-1), dn_h.reshape(-1)]).astype(I32)

    y_rows = _experts(row_tok, item_e, item_row0, item_nt, item_fill, n_items.reshape(1).astype(I32), blk_idx,
                      hn2p, w_gate[0], w_up[0], w_down[0], r_alloc)
    w_tok = jnp.pad(of[0:2].T, ((0, 0), (0, LANES - 2)))
    out = _combine(dst[0], dst[1], h1, w_tok, y_rows)
    return out.reshape(b, s, D_MODEL)
```

```python
import functools

import jax
import jax.numpy as jnp
import numpy as np
from jax import lax
from jax.experimental import pallas as pl
from jax.experimental.pallas import tpu as pltpu

F32 = jnp.float32
BF16 = jnp.bfloat16
I32 = jnp.int32
U32 = jnp.uint32

D_MODEL = 2048
SSD_HEADS = 32
SSD_HEAD_DIM = 64
SSD_D_INNER = 2048
SSD_GROUPS = 4
SSD_D_STATE = 128
SSD_CONV = 4
SSD_BC = SSD_GROUPS * SSD_D_STATE
SSD_CONV_DIM = SSD_D_INNER + 2 * SSD_BC
ATTN_HEADS = 16
ATTN_HEAD_DIM = 64
ATTN_D = 1024
ATTN_SPAN = 128
DILATIONS = (1, 4, 16)
ROPE_DIM = 16
ROPE_THETA = 500000.0
N_GROUPS = 8
EXPERTS_PER_GROUP = 8
N_EXPERTS = 64
D_EXPERT = 1024
EPS = 1e-6
NEG = -1e30

LANES = 128
VMEM_LIMIT = 56 * 1024 * 1024

COL_Z, COL_XS, COL_B, COL_C, COL_Q, COL_K, COL_V = 0, 2048, 4096, 4608, 5120, 6144, 7168
PROJ_COLS = 8192

SSD_L = 128
ATT_BLK = 2048
MOE_TM = 256


def _cparams(sem):
    return pltpu.CompilerParams(dimension_semantics=sem, vmem_limit_bytes=VMEM_LIMIT)


def _silu(v):
    return v * (1.0 / (1.0 + jnp.exp(-v)))


def _rope_body(pos_ref, f_ref, sg_ref, cos_ref, sin_ref):
    ang = pos_ref[...].astype(F32) * f_ref[...]
    cos_ref[...] = jnp.cos(ang)
    sin_ref[...] = jnp.sin(ang) * sg_ref[...]


def _rope_table(pos_col, f_lane, sg_lane):
    t = pos_col.shape[0]
    tt = min(t, 2048)
    return pl.pallas_call(
        _rope_body,
        grid=(t // tt,),
        in_specs=[pl.BlockSpec((tt, 1), lambda i: (i, 0)),
                  pl.BlockSpec((1, LANES), lambda i: (0, 0)),
                  pl.BlockSpec((1, LANES), lambda i: (0, 0))],
        out_specs=[pl.BlockSpec((tt, LANES), lambda i: (i, 0))] * 2,
        out_shape=[jax.ShapeDtypeStruct((t, LANES), F32)] * 2,
        compiler_params=_cparams(("arbitrary",)),
        name="rope_table",
    )(pos_col, f_lane, sg_lane)


def _inproj_body(x_ref, nw_ref, wa_ref, wb_ref, wdt_ref, proj_ref, dt_ref, hn_ref, *, tm, rc, n_a):
    j = pl.program_id(1)

    @pl.when(j == 0)
    def _():
        def chunk(c, carry):
            r = pl.multiple_of(c * rc, rc)
            xf = x_ref[pl.ds(r, rc), :]
            ms = jnp.mean(xf * xf, axis=-1, keepdims=True)
            hn_ref[pl.ds(r, rc), :] = (xf * lax.rsqrt(ms + EPS) * nw_ref[...]).astype(BF16)
            return carry

        lax.fori_loop(0, tm // rc, chunk, 0)
        dt_ref[...] = jnp.dot(hn_ref[...], wdt_ref[...], preferred_element_type=F32)

    @pl.when(j < n_a)
    def _():
        proj_ref[...] = jnp.dot(hn_ref[...], wa_ref[...], preferred_element_type=F32).astype(BF16)

    @pl.when(j >= n_a)
    def _():
        proj_ref[...] = jnp.dot(hn_ref[...], wb_ref[...], preferred_element_type=F32).astype(BF16)


def _inproj(x2, nw, w_all, w_qkv, wdt):
    t = x2.shape[0]
    tm, tn = min(t, 1024), 1024
    n_a = COL_Q // tn
    n_b = w_qkv.shape[1] // tn
    assert n_a * tn == COL_Q and n_a + n_b == PROJ_COLS // tn
    return pl.pallas_call(
        functools.partial(_inproj_body, tm=tm, rc=128, n_a=n_a),
        grid=(t // tm, PROJ_COLS // tn),
        in_specs=[pl.BlockSpec((tm, D_MODEL), lambda i, j: (i, 0)),
                  pl.BlockSpec((1, D_MODEL), lambda i, j: (0, 0)),
                  pl.BlockSpec((D_MODEL, tn), lambda i, j: (0, jnp.minimum(j, n_a - 1))),
                  pl.BlockSpec((D_MODEL, tn), lambda i, j: (0, jnp.clip(j - n_a, 0, n_b - 1))),
                  pl.BlockSpec((D_MODEL, LANES), lambda i, j: (0, 0))],
        out_specs=[pl.BlockSpec((tm, tn), lambda i, j: (i, j)),
                   pl.BlockSpec((tm, LANES), lambda i, j: (i, 0))],
        out_shape=[jax.ShapeDtypeStruct((t, PROJ_COLS), BF16),
                   jax.ShapeDtypeStruct((t, LANES), F32)],
        scratch_shapes=[pltpu.VMEM((tm, D_MODEL), BF16)],
        compiler_params=_cparams(("arbitrary", "arbitrary")),
        name="inproj",
    )(x2, nw, w_all, w_qkv, wdt)


def _ssd_body(z_ref, xs_ref, b_ref, c_ref, dtr_ref, cw_ref, cb_ref, dtb_ref, alog_ref, dsk_ref, nw_ref,
              hexp_ref, shift_ref, y_ref, ubuf, xc, bc, cc, ybuf, s_ref, *, L):
    ci = pl.program_id(0)

    @pl.when(ci == 0)
    def _():
        ubuf[0:CONV_HALO, :] = jnp.zeros((CONV_HALO, SSD_CONV_DIM), BF16)
        s_ref[...] = jnp.zeros_like(s_ref)

    ubuf[CONV_HALO:CONV_HALO + L, 0:SSD_D_INNER] = xs_ref[...]
    ubuf[CONV_HALO:CONV_HALO + L, SSD_D_INNER:SSD_D_INNER + SSD_BC] = b_ref[...]
    ubuf[CONV_HALO:CONV_HALO + L, SSD_D_INNER + SSD_BC:SSD_CONV_DIM] = c_ref[...]
    cw = 512
    for cch in range(SSD_CONV_DIM // cw):
        cs_ = slice(cw * cch, cw * cch + cw)
        taps = jnp.dot(shift_ref[...], ubuf[:, cs_], preferred_element_type=F32)
        acc = cb_ref[:, cs_] + cw_ref[0:1, cs_] * taps[0:L]
        for k in range(1, SSD_CONV):
            acc = acc + cw_ref[k:k + 1, cs_] * taps[k * L:(k + 1) * L]
        act = _silu(acc)
        if cch < 4:
            xc[:, cs_] = act
        elif cch == 4:
            bc[...] = act
        else:
            cc[...] = act
    ubuf[0:CONV_HALO, :] = ubuf[L:L + CONV_HALO, :]

    dt_in = dtr_ref[...] + dtb_ref[...]
    dt = jnp.maximum(dt_in, 0.0) + jnp.log1p(jnp.exp(-jnp.abs(dt_in)))
    a = dt * (-jnp.exp(alog_ref[...]))
    row = lax.broadcasted_iota(I32, (L, L), 0)
    col = lax.broadcasted_iota(I32, (L, L), 1)
    causal = col <= row
    cs = jnp.dot(causal.astype(F32), a, precision=lax.Precision.HIGHEST, preferred_element_type=F32)
    cs_last = cs[L - 1:L, :]
    wmat = jnp.exp(cs_last - cs) * dt
    cs_t = cs.T
    dt_t = dt.T
    cdec = jnp.broadcast_to(jnp.exp(cs_last), (16, LANES))
    c1 = cdec.astype(BF16)
    r1 = cdec - c1.astype(F32)
    c2 = r1.astype(BF16)
    c3 = (r1 - c2.astype(F32)).astype(BF16)
    cdec_x = (jnp.dot(c1, hexp_ref[...], preferred_element_type=F32)
              + jnp.dot(c2, hexp_ref[...], preferred_element_type=F32)
              + jnp.dot(c3, hexp_ref[...], preferred_element_type=F32))[0:1, :]
    lane = lax.broadcasted_iota(I32, (L, LANES), 1)
    first = lane < SSD_HEAD_DIM

    for g in range(SSD_GROUPS):
        bg = bc[:, LANES * g:LANES * g + LANES]
        cg = cc[:, LANES * g:LANES * g + LANES]
        cb = lax.dot_general(cg.astype(BF16), bg.astype(BF16), (((1,), (1,)), ((), ())),
                             preferred_element_type=F32)
        bg_t = bg.T.astype(BF16)
        xw_parts = []
        for q in range(4):
            lanes_ = slice(512 * g + LANES * q, 512 * g + LANES * q + LANES)
            x_pair = xc[:, lanes_]
            s_pair = s_ref[g, :, LANES * q:LANES * q + LANES]
            rhs = jnp.concatenate([x_pair.astype(BF16), s_pair.astype(BF16)], axis=0)
            ys, wbs = [], []
            for e2 in range(2):
                h = 8 * g + 2 * q + e2
                cs_col = jnp.broadcast_to(cs[:, h:h + 1], (L, L))
                lm = jnp.exp(jnp.where(causal, cs_col - cs_t[h:h + 1, :], NEG))
                m = cb * lm * dt_t[h:h + 1, :]
                e_col = jnp.exp(jnp.broadcast_to(cs[:, h:h + 1], (L, LANES)))
                lhs = jnp.concatenate([m.astype(BF16), (cg * e_col).astype(BF16)], axis=1)
                ys.append(jnp.dot(lhs, rhs, preferred_element_type=F32))
                wbs.append(jnp.broadcast_to(wmat[:, h:h + 1], (L, LANES)))
            ybuf[:, lanes_] = jnp.where(first, ys[0], ys[1])
            xw_parts.append((x_pair * jnp.where(first, wbs[0], wbs[1])).astype(BF16))
        xw_g = jnp.concatenate(xw_parts, axis=1)
        s_ref[g] = (s_ref[g] * cdec_x[:, 512 * g:512 * g + 512]
                    + jnp.dot(bg_t, xw_g, preferred_element_type=F32))

    for g in range(SSD_GROUPS):
        gs = slice(512 * g, 512 * g + 512)
        zf = z_ref[:, gs].astype(F32)
        yg = (ybuf[:, gs] + dsk_ref[:, gs] * xc[:, gs]) * _silu(zf)
        ms = jnp.mean(yg * yg, axis=-1, keepdims=True)
        y_ref[:, gs] = (yg * lax.rsqrt(ms + EPS) * nw_ref[:, gs]).astype(BF16)


CONV_HALO = 16


def _conv_shift_matrix(L):
    s = np.zeros((SSD_CONV * L, CONV_HALO + L), np.float32)
    for k in range(SSD_CONV):
        s[k * L + np.arange(L), CONV_HALO + np.arange(L) - (SSD_CONV - 1) + k] = 1.0
    return jnp.asarray(s, BF16)


def _ssd(proj, dtraw, cw, cb, dtb, alog, dsk, nw, hexp):
    t = proj.shape[0]
    L = SSD_L
    shift = _conv_shift_matrix(L)
    full = lambda shape: pl.BlockSpec(shape, lambda i: (0,) * len(shape))
    return pl.pallas_call(
        functools.partial(_ssd_body, L=L),
        grid=(t // L,),
        in_specs=[pl.BlockSpec((L, SSD_D_INNER), lambda i: (i, COL_Z // SSD_D_INNER)),
                  pl.BlockSpec((L, SSD_D_INNER), lambda i: (i, COL_XS // SSD_D_INNER)),
                  pl.BlockSpec((L, SSD_BC), lambda i: (i, COL_B // SSD_BC)),
                  pl.BlockSpec((L, SSD_BC), lambda i: (i, COL_C // SSD_BC)),
                  pl.BlockSpec((L, LANES), lambda i: (i, 0)),
                  full((SSD_CONV, SSD_CONV_DIM)), full((1, SSD_CONV_DIM)),
                  full((1, LANES)), full((1, LANES)),
                  full((1, SSD_D_INNER)), full((1, SSD_D_INNER)), full((LANES, SSD_D_INNER)),
                  full((SSD_CONV * L, CONV_HALO + L))],
        out_specs=pl.BlockSpec((L, SSD_D_INNER), lambda i: (i, 0)),
        out_shape=jax.ShapeDtypeStruct((t, SSD_D_INNER), BF16),
        scratch_shapes=[pltpu.VMEM((CONV_HALO + L, SSD_CONV_DIM), BF16),
                        pltpu.VMEM((L, SSD_D_INNER), F32),
                        pltpu.VMEM((L, SSD_BC), F32),
                        pltpu.VMEM((L, SSD_BC), F32),
                        pltpu.VMEM((L, SSD_D_INNER), F32),
                        pltpu.VMEM((SSD_GROUPS, SSD_D_STATE, 512), F32)],
        compiler_params=_cparams(("arbitrary",)),
        name="ssd_scan",
    )(proj, proj, proj, proj, dtraw, cw, cb, dtb, alog, dsk, nw, hexp, shift)


def _attn_body(q_ref, k_ref, v_ref, cos_ref, sin_ref, qw_ref, kw_ref, hsum_ref, bias_ref, o_ref,
               q_s, k_s, v_s, o_b, m_b, l_b, *, blk):
    j = pl.program_id(1)
    cur = (j % 2) * blk
    lane = lax.broadcasted_iota(I32, (1, LANES), 1)
    c64 = lane % ATTN_HEAD_DIM
    low = c64 < (ROPE_DIM // 2)

    def norm_rope(raw, w_lane):
        xf = raw.astype(F32)
        x2 = xf * xf
        hi = x2.astype(BF16)
        lo = (x2 - hi.astype(F32)).astype(BF16)
        ss = (jnp.dot(hi, hsum_ref[...], preferred_element_type=F32)
              + jnp.dot(lo, hsum_ref[...], preferred_element_type=F32))
        y = xf * lax.rsqrt(ss * (1.0 / ATTN_HEAD_DIM) + EPS) * w_lane
        partner = jnp.where(low, pltpu.roll(y, LANES - ROPE_DIM // 2, 1), pltpu.roll(y, ROPE_DIM // 2, 1))
        return y * cos_ref[...] + partner * sin_ref[...]

    @pl.when(j == 0)
    def _():
        k_s[pl.ds(blk, blk), :] = jnp.zeros((blk, LANES), F32)
        v_s[pl.ds(blk, blk), :] = jnp.zeros((blk, LANES), F32)

    q_s[...] = norm_rope(q_ref[...], qw_ref[...]) * (ATTN_HEAD_DIM ** -0.5)
    k_s[pl.ds(cur, blk), :] = norm_rope(k_ref[...], kw_ref[...])
    v_s[pl.ds(cur, blk), :] = v_ref[...].astype(F32)
    lane_q = lax.broadcasted_iota(I32, (ATTN_SPAN, LANES), 1)
    first = lane_q < ATTN_HEAD_DIM
    mask0 = first.astype(F32)
    mask1 = 1.0 - mask0
    ones_v = jnp.ones((2 * ATTN_SPAN, LANES), BF16)

    def rows(ref, start, d):
        if d == 1:
            return ref[pl.ds(start, ATTN_SPAN), :]
        return ref[pl.ds(start, ATTN_SPAN, stride=d), :]

    def tile(br, start, d):
        qv = rows(q_s, start, d)
        qs = jnp.concatenate([qv * mask0, qv * mask1], axis=0).astype(BF16)
        prev_start = (cur + start - ATTN_SPAN * d) & (2 * blk - 1)
        own_start = cur + start
        kt = jnp.concatenate([rows(k_s, prev_start, d), rows(k_s, own_start, d)], axis=0).astype(BF16)
        vt = jnp.concatenate([rows(v_s, prev_start, d), rows(v_s, own_start, d)], axis=0).astype(BF16)
        s = lax.dot_general(qs, kt, (((1,), (1,)), ((), ())), preferred_element_type=F32)
        no_prev = jnp.logical_and(j == 0, start < ATTN_SPAN * d)
        s = s + bias_ref[no_prev.astype(I32)]
        m2 = jnp.max(s, axis=-1, keepdims=True)
        p = jnp.exp(s - m2).astype(BF16)
        o2 = jnp.dot(p, jnp.concatenate([vt, ones_v], axis=1), preferred_element_type=F32)
        m_rep = jnp.broadcast_to(m2, (2 * ATTN_SPAN, LANES))
        if d == 1:
            sl = pl.ds(start, ATTN_SPAN)
        else:
            sl = pl.ds(start, ATTN_SPAN, stride=d)
        o_b[br, sl, :] = jnp.where(first, o2[:ATTN_SPAN, :LANES], o2[ATTN_SPAN:, :LANES])
        l_b[br, sl, :] = jnp.where(first, o2[:ATTN_SPAN, LANES:], o2[ATTN_SPAN:, LANES:])
        m_b[br, sl, :] = jnp.where(first, m_rep[:ATTN_SPAN], m_rep[ATTN_SPAN:])

    for br, d in enumerate(DILATIONS):
        n_res = d
        n_sub = blk // (d * ATTN_SPAN)

        def loop_body(i, carry, br=br, d=d, n_res=n_res):
            r = i % n_res
            sub = i // n_res
            tile(br, r + sub * (d * ATTN_SPAN), d)
            return carry

        lax.fori_loop(0, n_res * n_sub, loop_body, 0, unroll=8)

    def merge(c, carry):
        rs = pl.ds(pl.multiple_of(c * ATTN_SPAN, ATTN_SPAN), ATTN_SPAN)
        ms = [m_b[br, rs, :] for br in range(len(DILATIONS))]
        m_all = jnp.maximum(jnp.maximum(ms[0], ms[1]), ms[2])
        ws = [jnp.exp(m - m_all) for m in ms]
        num = ws[0] * o_b[0, rs, :] + ws[1] * o_b[1, rs, :] + ws[2] * o_b[2, rs, :]
        den = ws[0] * l_b[0, rs, :] + ws[1] * l_b[1, rs, :] + ws[2] * l_b[2, rs, :]
        o_ref[rs, :] = (num / den).astype(BF16)
        return carry

    lax.fori_loop(0, blk // ATTN_SPAN, merge, 0)


def _attention(proj, cos_t, sin_t, qw_lane, kw_lane, hsum, bias):
    t = proj.shape[0]
    blk = ATT_BLK
    n_hp = ATTN_HEADS // 2
    colblk = lambda base: (lambda hp, j: (j, base // LANES + hp))
    full = lambda shape: pl.BlockSpec(shape, lambda hp, j: (0,) * len(shape))
    return pl.pallas_call(
        functools.partial(_attn_body, blk=blk),
        grid=(n_hp, t // blk),
        in_specs=[pl.BlockSpec((blk, LANES), colblk(COL_Q)),
                  pl.BlockSpec((blk, LANES), colblk(COL_K)),
                  pl.BlockSpec((blk, LANES), colblk(COL_V)),
                  pl.BlockSpec((blk, LANES), lambda hp, j: (j, 0)),
                  pl.BlockSpec((blk, LANES), lambda hp, j: (j, 0)),
                  full((1, LANES)), full((1, LANES)), full((LANES, LANES)),
                  full((2, 2 * ATTN_SPAN, 2 * ATTN_SPAN))],
        out_specs=pl.BlockSpec((blk, LANES), lambda hp, j: (j, hp)),
        out_shape=jax.ShapeDtypeStruct((t, ATTN_D), BF16),
        scratch_shapes=[pltpu.VMEM((blk, LANES), F32),
                        pltpu.VMEM((2 * blk, LANES), F32),
                        pltpu.VMEM((2 * blk, LANES), F32),
                        pltpu.VMEM((len(DILATIONS), blk, LANES), F32),
                        pltpu.VMEM((len(DILATIONS), blk, LANES), F32),
                        pltpu.VMEM((len(DILATIONS), blk, LANES), F32)],
        compiler_params=_cparams(("arbitrary", "arbitrary")),
        name="dilated_attn",
    )(proj, proj, proj, cos_t, sin_t, qw_lane, kw_lane, hsum, bias)


def _pack_bf16_pair(lo_f32, hi_f32):
    lo_bits = pltpu.bitcast(lo_f32.astype(BF16).astype(F32), U32)
    hi_bits = pltpu.bitcast(hi_f32.astype(BF16).astype(F32), U32)
    return (lo_bits >> 16) | (hi_bits & jnp.uint32(0xFFFF0000))


def _unpack_bf16_pair(w):
    lo = pltpu.bitcast(w << 16, F32)
    hi = pltpu.bitcast(w & jnp.uint32(0xFFFF0000), F32)
    return lo, hi


def _outproj_body(ys_ref, ya_ref, x_ref, w_ref, nw_ref, wr_ref, h_ref, hn_ref, lg_ref):
    acc = jnp.dot(ys_ref[...], w_ref[0:SSD_D_INNER, :], preferred_element_type=F32)
    acc = acc + jnp.dot(ya_ref[...], w_ref[SSD_D_INNER:SSD_D_INNER + ATTN_D, :], preferred_element_type=F32)
    h = x_ref[...] + acc
    h_ref[...] = h
    ms = jnp.mean(h * h, axis=-1, keepdims=True)
    hn = h * lax.rsqrt(ms + EPS) * nw_ref[...]
    half = D_MODEL // 2
    hn_ref[...] = _pack_bf16_pair(hn[:, :half], hn[:, half:])
    hn_hi = hn.astype(BF16)
    hn_lo = (hn - hn_hi.astype(F32)).astype(BF16)
    nt_dims = (((1,), (1,)), ((), ()))
    lg_ref[...] = (lax.dot_general(wr_ref[0], hn_hi, nt_dims, preferred_element_type=F32)
                   + lax.dot_general(wr_ref[0], hn_lo, nt_dims, preferred_element_type=F32)
                   + lax.dot_general(wr_ref[1], hn_hi, nt_dims, preferred_element_type=F32))


def _outproj(yssd, yattn, x2, w, nw, wr_t):
    t = x2.shape[0]
    tm = 512
    full = lambda shape: pl.BlockSpec(shape, lambda i: (0,) * len(shape))
    return pl.pallas_call(
        _outproj_body,
        grid=(t // tm,),
        in_specs=[pl.BlockSpec((tm, SSD_D_INNER), lambda i: (i, 0)),
                  pl.BlockSpec((tm, ATTN_D), lambda i: (i, 0)),
                  pl.BlockSpec((tm, D_MODEL), lambda i: (i, 0)),
                  full((SSD_D_INNER + ATTN_D, D_MODEL)), full((1, D_MODEL)), full((2, LANES, D_MODEL))],
        out_specs=[pl.BlockSpec((tm, D_MODEL), lambda i: (i, 0)),
                   pl.BlockSpec((tm, D_MODEL // 2), lambda i: (i, 0)),
                   pl.BlockSpec((LANES, tm), lambda i: (0, i))],
        out_shape=[jax.ShapeDtypeStruct((t, D_MODEL), F32),
                   jax.ShapeDtypeStruct((t, D_MODEL // 2), U32),
                   jax.ShapeDtypeStruct((LANES, t), F32)],
        compiler_params=_cparams(("arbitrary",)),
        name="outproj",
    )(yssd, yattn, x2, w, nw, wr_t)


def _route_body(lg_ref, bias_ref, u_ref, oi_ref, of_ref, cnt_ref, carry, *, tt):
    i = pl.program_id(0)

    @pl.when(i == 0)
    def _():
        carry[...] = jnp.zeros_like(carry)

    lg = lg_ref[...] + bias_ref[...]
    sub8 = lax.broadcasted_iota(I32, (8, tt), 0)
    g = lg[0:8]
    gmax = jnp.max(g, axis=0, keepdims=True)
    gidx = jnp.min(jnp.where(g == gmax, sub8, 8), axis=0, keepdims=True)
    gval = 1.0 / jnp.sum(jnp.exp(g - gmax), axis=0, keepdims=True)
    esel = jnp.zeros((8, tt), F32)
    for grp in range(N_GROUPS):
        esel = jnp.where(gidx == grp, lg[8 + 8 * grp:16 + 8 * grp], esel)
    m1 = jnp.max(esel, axis=0, keepdims=True)
    i1 = jnp.min(jnp.where(esel == m1, sub8, 8), axis=0, keepdims=True)
    em = jnp.where(sub8 == i1, -jnp.inf, esel)
    m2 = jnp.max(em, axis=0, keepdims=True)
    i2 = jnp.min(jnp.where(em == m2, sub8, 8), axis=0, keepdims=True)
    r = jnp.exp(m2 - m1)
    w1 = gval * (1.0 / (1.0 + r))
    w2 = gval * (r / (1.0 + r))
    e1 = gidx * EXPERTS_PER_GROUP + i1
    e2 = gidx * EXPERTS_PER_GROUP + i2
    sub64 = lax.broadcasted_iota(I32, (N_EXPERTS, tt), 0)
    oh1 = sub64 == e1
    oh2 = sub64 == e2
    oh = oh1.astype(F32) + oh2.astype(F32)
    pref = jnp.dot(oh.astype(BF16), u_ref[...], preferred_element_type=F32)
    excl = carry[...] + pref - 1.0
    rank1 = jnp.sum(jnp.where(oh1, excl, 0.0), axis=0, keepdims=True)
    rank2 = jnp.sum(jnp.where(oh2, excl, 0.0), axis=0, keepdims=True)
    carry[...] = carry[...] + pref[:, tt - 1:tt]
    zi = jnp.zeros((4, tt), I32)
    oi_ref[...] = jnp.concatenate([e1, e2, rank1.astype(I32), rank2.astype(I32), zi], axis=0)
    of_ref[...] = jnp.concatenate([w1, w2, jnp.zeros((6, tt), F32)], axis=0)
    cnt_ref[...] = jnp.broadcast_to(carry[...], (N_EXPERTS, LANES))


def _route(lg_t, bias_col, utri):
    t = lg_t.shape[1]
    tt = utri.shape[0]
    return pl.pallas_call(
        functools.partial(_route_body, tt=tt),
        grid=(t // tt,),
        in_specs=[pl.BlockSpec((LANES, tt), lambda i: (0, i)),
                  pl.BlockSpec((LANES, 1), lambda i: (0, 0)),
                  pl.BlockSpec((tt, tt), lambda i: (0, 0))],
        out_specs=[pl.BlockSpec((8, tt), lambda i: (0, i)),
                   pl.BlockSpec((8, tt), lambda i: (0, i)),
                   pl.BlockSpec((N_EXPERTS, LANES), lambda i: (0, 0))],
        out_shape=[jax.ShapeDtypeStruct((8, t), I32),
                   jax.ShapeDtypeStruct((8, t), F32),
                   jax.ShapeDtypeStruct((N_EXPERTS, LANES), F32)],
        scratch_shapes=[pltpu.VMEM((N_EXPERTS, 1), F32)],
        compiler_params=_cparams(("arbitrary",)),
        name="route",
    )(lg_t, bias_col, utri)


def _dest_body(oi_ref, poff_ref, d_ref, *, tt):
    sub64 = lax.broadcasted_iota(I32, (N_EXPERTS, tt), 0)
    oi = oi_ref[...]
    poff = poff_ref[...]
    d1 = jnp.sum(jnp.where(sub64 == oi[0:1], poff, 0), axis=0, keepdims=True) + oi[2:3]
    d2 = jnp.sum(jnp.where(sub64 == oi[1:2], poff, 0), axis=0, keepdims=True) + oi[3:4]
    d_ref[...] = jnp.concatenate([d1, d2, jnp.zeros((6, tt), I32)], axis=0)


def _dest(oi, poff_col):
    t = oi.shape[1]
    tt = min(t, 2048)
    return pl.pallas_call(
        functools.partial(_dest_body, tt=tt),
        grid=(t // tt,),
        in_specs=[pl.BlockSpec((8, tt), lambda i: (0, i)),
                  pl.BlockSpec((N_EXPERTS, 1), lambda i: (0, 0))],
        out_specs=pl.BlockSpec((8, tt), lambda i: (0, i)),
        out_shape=jax.ShapeDtypeStruct((8, t), I32),
        compiler_params=_cparams(("arbitrary",)),
        name="dest_rows",
    )(oi, poff_col)


def _rowtok_body(d1_ref, d2_ref, rt_ref, *, t, r_alloc):
    def clear(i, c):
        rt_ref[i] = 0
        return c

    lax.fori_loop(0, r_alloc, clear, 0, unroll=8)

    def put(i, c):
        rt_ref[d1_ref[i]] = i
        rt_ref[d2_ref[i]] = i
        return c

    lax.fori_loop(0, t, put, 0, unroll=8)


def _rowtok(d1, d2, r_alloc):
    t = d1.shape[0]
    smem = pl.BlockSpec(memory_space=pltpu.SMEM)
    return pl.pallas_call(
        functools.partial(_rowtok_body, t=t, r_alloc=r_alloc),
        in_specs=[smem, smem],
        out_specs=smem,
        out_shape=jax.ShapeDtypeStruct((r_alloc,), I32),
        name="row_tokens",
    )(d1, d2)


def _experts_body(rt_ref, ie_ref, ir_ref, int_ref, ifl_ref, ni_ref, bi_ref, x_hbm,
                  wga_ref, wgb_ref, wua_ref, wub_ref, wdla_ref, wdlb_ref, wdha_ref, wdhb_ref,
                  y_hbm, xlo, xhi, hid, ybuf, ring, gsem, ysem, *, nsub):
    w = pl.program_id(0)
    s = pl.program_id(1)
    nt = int_ref[w]
    nfill = ifl_ref[w]
    row0 = ir_ref[w]
    tile0 = row0 // MOE_TM
    n_items = ni_ref[0]
    half = D_MODEL // 2
    hn = D_EXPERT // nsub

    def gather_quarter(tile, q):
        slot = tile % RING_TILES
        base = tile * MOE_TM + q * GATHER_GROUP
        for j in range(GATHER_GROUP):
            tok = rt_ref[base + j]
            pltpu.make_async_copy(x_hbm.at[pl.ds(tok, 1)], ring.at[slot, pl.ds(q * GATHER_GROUP + j, 1)],
                                  gsem.at[slot]).start()

    def ring_wait(tile):
        slot = tile % RING_TILES
        pltpu.make_async_copy(x_hbm.at[pl.ds(0, MOE_TM)], ring.at[slot], gsem.at[slot]).wait()

    def tile_copy_out(r0, tl):
        rs = pl.ds(pl.multiple_of(tl * MOE_TM, MOE_TM), MOE_TM)
        dst = pl.ds(pl.multiple_of(r0 + tl * MOE_TM, MOE_TM), MOE_TM)
        return pltpu.make_async_copy(ybuf.at[rs], y_hbm.at[dst], ysem)

    def for_tiles(n, fn):
        def body(tl, c):
            fn(tl)
            return c

        lax.fori_loop(0, n, body, 0)

    @pl.when(jnp.logical_and(w == 0, s == 0))
    def _():
        for q in range(GROUPS_PER_TILE):
            for_tiles(ITEM_TILES, lambda tl, q=q: gather_quarter(tl, q))

    @pl.when(jnp.logical_and(nt > 0, s == 0))
    def _():
        for_tiles(nt, lambda tl: ring_wait(tile0 + tl))

    def gateup(tl, step):
        rs = pl.ds(pl.multiple_of(tl * MOE_TM, MOE_TM), MOE_TM)
        if step == 0:
            lo, hi = _unpack_bf16_pair(ring[(tile0 + tl) % RING_TILES])
            a = lo.astype(BF16)
            b = hi.astype(BF16)
            xlo[rs, :] = a
            xhi[rs, :] = b
        else:
            a = xlo[rs, :]
            b = xhi[rs, :]
        gate = (jnp.dot(a, wga_ref[...].astype(BF16), preferred_element_type=F32)
                + jnp.dot(b, wgb_ref[...].astype(BF16), preferred_element_type=F32))
        up = (jnp.dot(a, wua_ref[...].astype(BF16), preferred_element_type=F32)
              + jnp.dot(b, wub_ref[...].astype(BF16), preferred_element_type=F32))
        hid[rs, step * hn:(step + 1) * hn] = (_silu(gate) * up).astype(BF16)
        gather_quarter(tile0 + ITEM_TILES + tl, step)

    def down(tl, step):
        rs = pl.ds(pl.multiple_of(tl * MOE_TM, MOE_TM), MOE_TM)
        gather_quarter(tile0 + ITEM_TILES + tl, nsub + step)
        ha = hid[rs, 0:D_EXPERT // 2]
        hb = hid[rs, D_EXPERT // 2:D_EXPERT]
        ybuf[rs, step * hn:(step + 1) * hn] = _pack_bf16_pair(
            jnp.dot(ha, wdla_ref[...].astype(BF16), preferred_element_type=F32)
            + jnp.dot(hb, wdlb_ref[...].astype(BF16), preferred_element_type=F32),
            jnp.dot(ha, wdha_ref[...].astype(BF16), preferred_element_type=F32)
            + jnp.dot(hb, wdhb_ref[...].astype(BF16), preferred_element_type=F32))

    for step in range(nsub):
        @pl.when(jnp.logical_and(nt > 0, s == step))
        def _(step=step):
            for_tiles(nt, lambda tl: gateup(tl, step))

    @pl.when(jnp.logical_and(jnp.logical_and(nt > 0, w > 0), s == nsub))
    def _():
        for_tiles(int_ref[jnp.maximum(w - 1, 0)], lambda tl: tile_copy_out(row0, tl).wait())

    for step in range(nsub):
        @pl.when(jnp.logical_and(nt > 0, s == nsub + step))
        def _(step=step):
            for_tiles(nt, lambda tl: down(tl, step))

    @pl.when(jnp.logical_and(nt > 0, s == 2 * nsub - 1))
    def _():
        for_tiles(nt, lambda tl: tile_copy_out(row0, tl).start())

    @pl.when(jnp.logical_and(w == n_items - 1, s == 2 * nsub - 1))
    def _():
        for_tiles(nt, lambda tl: tile_copy_out(row0, tl).wait())
        for_tiles(ITEM_TILES, lambda tl: ring_wait(tile0 + nt + tl))

    @pl.when(jnp.logical_and(nfill > 0, s == 0))
    def _():
        ybuf[...] = jnp.zeros_like(ybuf)
        for_tiles(nfill, lambda tl: tile_copy_out(row0, tl).start())
        for_tiles(nfill, lambda tl: tile_copy_out(row0, tl).wait())


ITEM_TILES = 4
RING_TILES = 2 * ITEM_TILES
GATHER_GROUP = 64
GROUPS_PER_TILE = MOE_TM // GATHER_GROUP


def _experts(row_tok, item_e, item_row0, item_nt, item_fill, n_items, blk_idx, hn2p, w_gate, w_up, w_down, r_alloc):
    ni = item_e.shape[0]
    nsub = 2
    assert GROUPS_PER_TILE == 2 * nsub
    hn = D_EXPERT // nsub
    rows = ITEM_TILES * MOE_TM
    nstep = 2 * nsub

    ns = ni * nstep


    def gu_map(kh):
        return lambda w, s, rt, ie, ir, nt, fl, n, bi: (bi[w * nstep + s], kh, bi[ns + w * nstep + s])

    def dl_map(kh):
        return lambda w, s, rt, ie, ir, nt, fl, n, bi: (bi[2 * ns + w * nstep + s], kh, bi[3 * ns + w * nstep + s])

    def dh_map(kh):
        return lambda w, s, rt, ie, ir, nt, fl, n, bi: (bi[2 * ns + w * nstep + s], kh,
                                                        nsub + bi[3 * ns + w * nstep + s])

    return pl.pallas_call(
        functools.partial(_experts_body, nsub=nsub),
        grid_spec=pltpu.PrefetchScalarGridSpec(
            num_scalar_prefetch=7,
            grid=(ni, nstep),
            in_specs=[pl.BlockSpec(memory_space=pl.ANY),
                      pl.BlockSpec((None, D_MODEL // 2, hn), gu_map(0)),
                      pl.BlockSpec((None, D_MODEL // 2, hn), gu_map(1)),
                      pl.BlockSpec((None, D_MODEL // 2, hn), gu_map(0)),
                      pl.BlockSpec((None, D_MODEL // 2, hn), gu_map(1)),
                      pl.BlockSpec((None, D_EXPERT // 2, hn), dl_map(0)),
                      pl.BlockSpec((None, D_EXPERT // 2, hn), dl_map(1)),
                      pl.BlockSpec((None, D_EXPERT // 2, hn), dh_map(0)),
                      pl.BlockSpec((None, D_EXPERT // 2, hn), dh_map(1))],
            out_specs=pl.BlockSpec(memory_space=pl.ANY),
            scratch_shapes=[pltpu.VMEM((rows, D_MODEL // 2), BF16),
                            pltpu.VMEM((rows, D_MODEL // 2), BF16),
                            pltpu.VMEM((rows, D_EXPERT), BF16),
                            pltpu.VMEM((rows, D_MODEL // 2), U32),
                            pltpu.VMEM((RING_TILES, MOE_TM, D_MODEL // 2), U32),
                            pltpu.SemaphoreType.DMA((RING_TILES,)),
                            pltpu.SemaphoreType.DMA(())]),
        out_shape=jax.ShapeDtypeStruct((r_alloc, D_MODEL // 2), U32),
        compiler_params=_cparams(("arbitrary", "arbitrary")),
        name="experts",
    )(row_tok, item_e, item_row0, item_nt, item_fill, n_items, blk_idx, hn2p,
      w_gate, w_gate, w_up, w_up, w_down, w_down, w_down, w_down)


def _combine_body(d1_ref, d2_ref, h_ref, w_ref, y_ref, o_ref, buf, sem, *, tt, n_steps):
    i = pl.program_id(0)

    def copies(step, slot, r):
        t0 = step * tt + r
        c1 = pltpu.make_async_copy(y_ref.at[pl.ds(d1_ref[t0], 1)], buf.at[slot, 0, pl.ds(r, 1)], sem.at[slot])
        c2 = pltpu.make_async_copy(y_ref.at[pl.ds(d2_ref[t0], 1)], buf.at[slot, 1, pl.ds(r, 1)], sem.at[slot])
        return c1, c2

    def issue(step, slot):
        def body(r, c):
            c1, c2 = copies(step, slot, r)
            c1.start()
            c2.start()
            return c

        lax.fori_loop(0, tt, body, 0, unroll=8)

    @pl.when(i == 0)
    def _():
        issue(0, 0)

    @pl.when(i + 1 < n_steps)
    def _():
        issue(i + 1, (i + 1) % 2)

    slot = i % 2
    pltpu.make_async_copy(y_ref.at[pl.ds(0, tt)], buf.at[slot, 0], sem.at[slot]).wait()
    pltpu.make_async_copy(y_ref.at[pl.ds(0, tt)], buf.at[slot, 1], sem.at[slot]).wait()
    a_lo, a_hi = _unpack_bf16_pair(buf[slot, 0])
    b_lo, b_hi = _unpack_bf16_pair(buf[slot, 1])
    w1 = w_ref[:, 0:1]
    w2 = w_ref[:, 1:2]
    half = D_MODEL // 2
    o_ref[:, 0:half] = h_ref[:, 0:half] + (a_lo * w1 + b_lo * w2)
    o_ref[:, half:D_MODEL] = h_ref[:, half:D_MODEL] + (a_hi * w1 + b_hi * w2)


def _combine(d1, d2, h1, w_tok, y_rows):
    t = h1.shape[0]
    tt = 256
    n_steps = t // tt
    return pl.pallas_call(
        functools.partial(_combine_body, tt=tt, n_steps=n_steps),
        grid_spec=pltpu.PrefetchScalarGridSpec(
            num_scalar_prefetch=2,
            grid=(n_steps,),
            in_specs=[pl.BlockSpec((tt, D_MODEL), lambda i, d1, d2: (i, 0)),
                      pl.BlockSpec((tt, LANES), lambda i, d1, d2: (i, 0)),
                      pl.BlockSpec(memory_space=pl.ANY)],
            out_specs=pl.BlockSpec((tt, D_MODEL), lambda i, d1, d2: (i, 0)),
            scratch_shapes=[pltpu.VMEM((2, 2, tt, D_MODEL // 2), U32),
                            pltpu.SemaphoreType.DMA((2,))]),
        out_shape=jax.ShapeDtypeStruct((t, D_MODEL), F32),
        compiler_params=_cparams(("arbitrary",)),
        name="moe_combine",
    )(d1, d2, h1, w_tok, y_rows)


def _band_bias():
    qi = np.arange(ATTN_SPAN)[:, None]
    kj = np.arange(ATTN_SPAN)[None, :]
    prev = np.where(kj >= qi, 0.0, NEG)
    own = np.where(kj <= qi, 0.0, NEG)
    with_prev = np.concatenate([prev, own], axis=1)
    no_prev = np.concatenate([np.full_like(prev, NEG), own], axis=1)
    both = np.stack([np.tile(with_prev, (2, 1)), np.tile(no_prev, (2, 1))])
    return jnp.asarray(both, F32)


def kernel(x, positions, norm1_w, w_in, conv_w, conv_b, dt_bias, A_log, D_skip, ssd_norm_w, q_norm_w, k_norm_w,
           w_out, norm2_w, w_group_router, b_group_router, w_expert_router, b_expert_router, w_gate, w_up, w_down):
    b, s, _ = x.shape
    assert b == 1 and norm1_w.shape[0] == 1
    t = s
    x2 = x.reshape(t, D_MODEL)

    w_in0 = w_in[0]
    zc, xc_, bc_, cc_, dtc, qc, kc, vc = np.cumsum((0, 2048, 2048, 512, 512, 32, 1024, 1024))
    assert dtc == COL_Q
    w_all = w_in0.astype(BF16)
    w_qkv = w_all[:, qc:]
    w_dt = jnp.pad(w_all[:, dtc:qc], ((0, 0), (0, LANES - SSD_HEADS)))
    pad_h = lambda v: jnp.pad(v.astype(F32), (0, LANES - SSD_HEADS)).reshape(1, LANES)
    lane = np.arange(LANES)
    c64 = lane % ATTN_HEAD_DIM
    inv_freq = 1.0 / (ROPE_THETA ** (jnp.arange(0, ROPE_DIM, 2, dtype=F32) / ROPE_DIM))
    f_lane = jnp.where(jnp.asarray(c64 < ROPE_DIM), inv_freq[jnp.asarray(c64 % (ROPE_DIM // 2))], 0.0).reshape(1, LANES)
    sg_lane = jnp.asarray(np.where(c64 < ROPE_DIM // 2, -1.0, np.where(c64 < ROPE_DIM, 1.0, 0.0)), F32).reshape(1, LANES)
    hsum = jnp.asarray((lane[:, None] // ATTN_HEAD_DIM) == (lane[None, :] // ATTN_HEAD_DIM), BF16)
    hexp = jnp.asarray(np.arange(LANES)[:, None] == (np.arange(SSD_D_INNER)[None, :] // SSD_HEAD_DIM), BF16)
    qw_lane = jnp.tile(q_norm_w[0].astype(F32), 2).reshape(1, LANES)
    kw_lane = jnp.tile(k_norm_w[0].astype(F32), 2).reshape(1, LANES)
    dsk = jnp.repeat(D_skip[0].astype(F32), SSD_HEAD_DIM).reshape(1, SSD_D_INNER)
    wr = jnp.concatenate([w_group_router[0],
                          jnp.transpose(w_expert_router[0], (1, 0, 2)).reshape(D_MODEL, N_EXPERTS)], axis=1)
    wr_f = jnp.pad(wr.T.astype(F32), ((0, LANES - N_GROUPS - N_EXPERTS), (0, 0)))
    wr_hi = wr_f.astype(BF16)
    wr_t = jnp.stack([wr_hi, (wr_f - wr_hi.astype(F32)).astype(BF16)])
    br = jnp.pad(jnp.concatenate([b_group_router[0], b_expert_router[0].reshape(-1)]).astype(F32),
                 (0, LANES - N_GROUPS - N_EXPERTS)).reshape(LANES, 1)

    cos_t, sin_t = _rope_table(positions.reshape(t, 1), f_lane, sg_lane)
    proj, dtraw = _inproj(x2, norm1_w.astype(F32), w_all, w_qkv, w_dt)
    y_ssd = _ssd(proj, dtraw, conv_w[0].astype(F32), conv_b.astype(F32), pad_h(dt_bias[0]), pad_h(A_log[0]),
                 dsk, ssd_norm_w.astype(F32), hexp)
    y_attn = _attention(proj, cos_t, sin_t, qw_lane, kw_lane, hsum, _band_bias())
    h1, hn2p, lg_t = _outproj(y_ssd, y_attn, x2, w_out[0].astype(BF16), norm2_w.astype(F32), wr_t)

    rt_tt = 512
    utri = jnp.asarray(np.arange(rt_tt)[:, None] <= np.arange(rt_tt)[None, :], BF16)
    oi, of, cnt = _route(lg_t, br, utri)
    counts = cnt[:, 0].astype(I32)
    n_tiles_max = (2 * t + N_EXPERTS * (MOE_TM - 1) + MOE_TM - 1) // MOE_TM
    r_alloc = n_tiles_max * MOE_TM
    tiles_e = (counts + MOE_TM - 1) // MOE_TM
    tile_end = jnp.cumsum(tiles_e)
    n_used = tile_end[-1]
    poff = ((tile_end - tiles_e) * MOE_TM).reshape(N_EXPERTS, 1)
    nt = n_used.reshape(1).astype(I32)
    dst = _dest(oi, poff.astype(I32))
    row_tok = _rowtok(dst[0], dst[1], r_alloc + ITEM_TILES * MOE_TM)

    items_e = (tiles_e + ITEM_TILES - 1) // ITEM_TILES
    item_end = jnp.cumsum(items_e)
    n_items = item_end[-1]
    ni_max = (n_tiles_max + (ITEM_TILES - 1) * N_EXPERTS + ITEM_TILES - 1) // ITEM_TILES
    w_idx = jnp.arange(ni_max, dtype=I32)
    w_live = w_idx < n_items
    w_c = jnp.minimum(w_idx, n_items - 1)
    item_e = jnp.searchsorted(item_end, w_c, side="right").astype(I32)
    k_in_e = w_c - (item_end - items_e)[item_e]
    spare = w_idx - n_items
    item_row0 = jnp.where(w_live, poff[item_e, 0] + k_in_e * (ITEM_TILES * MOE_TM),
                          (n_used + ITEM_TILES * spare) * MOE_TM).astype(I32)
    item_nt = jnp.where(w_live, jnp.clip(tiles_e[item_e] - ITEM_TILES * k_in_e, 0, ITEM_TILES), 0).astype(I32)
    item_fill = jnp.where(w_live, 0,
                          jnp.clip(n_tiles_max - n_used - ITEM_TILES * spare, 0, ITEM_TILES)).astype(I32)

    e_last = item_e[jnp.maximum(n_items - 1, 0)]
    e_next = jnp.where(w_idx + 1 < n_items, item_e[jnp.minimum(w_idx + 1, ni_max - 1)], e_last)
    e_prev = item_e[jnp.maximum(w_idx - 1, 0)]
    gu_e = jnp.stack([item_e, item_e, e_next, e_next], axis=1)
    gu_h = jnp.broadcast_to(jnp.asarray([0, 1, 0, 0], I32), (ni_max, 4))
    dn_e = jnp.stack([e_prev, e_prev, item_e, item_e], axis=1)
    dn_h = jnp.where((w_idx == 0)[:, None], jnp.asarray([0, 0, 0, 1], I32), jnp.asarray([1, 1, 0, 1], I32))
    dead = jnp.logical_not(w_live)[:, None]
    gu_e = jnp.where(dead, e_last, gu_e)
    gu_h = jnp.where(dead, 0, gu_h)
    dn_e = jnp.where(dead, e_last, dn_e)
    dn_h = jnp.where(dead, 1, dn_h)
    blk_idx = jnp.concatenate([gu_e.reshape(-1), gu_h.reshape(-1), dn_e.reshape(-1), dn_h.reshape(-1)]).astype(I32)

    y_rows = _experts(row_tok, item_e, item_row0, item_nt, item_fill, n_items.reshape(1).astype(I32), blk_idx,
                      hn2p, w_gate[0], w_up[0], w_down[0], r_alloc)
    w_tok = jnp.pad(of[0:2].T, ((0, 0), (0, LANES - 2)))
    out = _combine(dst[0], dst[1], h1, w_tok, y_rows)
    return out.reshape(b, s, D_MODEL)
```

```python
import functools

import jax
import jax.numpy as jnp
import numpy as np
from jax import lax
from jax.experimental import pallas as pl
from jax.experimental.pallas import tpu as pltpu

F32 = jnp.float32
BF16 = jnp.bfloat16
I32 = jnp.int32
U32 = jnp.uint32

D_MODEL = 2048
SSD_HEADS = 32
SSD_HEAD_DIM = 64
SSD_D_INNER = 2048
SSD_GROUPS = 4
SSD_D_STATE = 128
SSD_CONV = 4
SSD_BC = SSD_GROUPS * SSD_D_STATE
SSD_CONV_DIM = SSD_D_INNER + 2 * SSD_BC
ATTN_HEADS = 16
ATTN_HEAD_DIM = 64
ATTN_D = 1024
ATTN_SPAN = 128
DILATIONS = (1, 4, 16)
ROPE_DIM = 16
ROPE_THETA = 500000.0
N_GROUPS = 8
EXPERTS_PER_GROUP = 8
N_EXPERTS = 64
D_EXPERT = 1024
EPS = 1e-6
NEG = -1e30

LANES = 128
VMEM_LIMIT = 56 * 1024 * 1024

COL_Z, COL_XS, COL_B, COL_C, COL_Q, COL_K, COL_V = 0, 2048, 4096, 4608, 5120, 6144, 7168
PROJ_COLS = 8192

SSD_L = 128
ATT_BLK = 2048
MOE_TM = 256


def _cparams(sem):
    return pltpu.CompilerParams(dimension_semantics=sem, vmem_limit_bytes=VMEM_LIMIT)


def _silu(v):
    return v * (1.0 / (1.0 + jnp.exp(-v)))


def _rope_body(pos_ref, f_ref, sg_ref, cos_ref, sin_ref):
    ang = pos_ref[...].astype(F32) * f_ref[...]
    cos_ref[...] = jnp.cos(ang)
    sin_ref[...] = jnp.sin(ang) * sg_ref[...]


def _rope_table(pos_col, f_lane, sg_lane):
    t = pos_col.shape[0]
    tt = min(t, 2048)
    return pl.pallas_call(
        _rope_body,
        grid=(t // tt,),
        in_specs=[pl.BlockSpec((tt, 1), lambda i: (i, 0)),
                  pl.BlockSpec((1, LANES), lambda i: (0, 0)),
                  pl.BlockSpec((1, LANES), lambda i: (0, 0))],
        out_specs=[pl.BlockSpec((tt, LANES), lambda i: (i, 0))] * 2,
        out_shape=[jax.ShapeDtypeStruct((t, LANES), F32)] * 2,
        compiler_params=_cparams(("arbitrary",)),
        name="rope_table",
    )(pos_col, f_lane, sg_lane)


def _inproj_body(x_ref, nw_ref, w_ref, wdt_ref, proj_ref, dt_ref, hn_ref, *, tm, rc):
    j = pl.program_id(1)

    @pl.when(j == 0)
    def _():
        def chunk(c, carry):
            r = pl.multiple_of(c * rc, rc)
            xf = x_ref[pl.ds(r, rc), :]
            ms = jnp.mean(xf * xf, axis=-1, keepdims=True)
            hn_ref[pl.ds(r, rc), :] = (xf * lax.rsqrt(ms + EPS) * nw_ref[...]).astype(BF16)
            return carry

        lax.fori_loop(0, tm // rc, chunk, 0)
        dt_ref[...] = jnp.dot(hn_ref[...], wdt_ref[...], preferred_element_type=F32)

    proj_ref[...] = jnp.dot(hn_ref[...], w_ref[...], preferred_element_type=F32).astype(BF16)


def _inproj(x2, nw, w, wdt):
    t = x2.shape[0]
    tm, tn = min(t, 1024), 1024
    return pl.pallas_call(
        functools.partial(_inproj_body, tm=tm, rc=128),
        grid=(t // tm, PROJ_COLS // tn),
        in_specs=[pl.BlockSpec((tm, D_MODEL), lambda i, j: (i, 0)),
                  pl.BlockSpec((1, D_MODEL), lambda i, j: (0, 0)),
                  pl.BlockSpec((D_MODEL, tn), lambda i, j: (0, j)),
                  pl.BlockSpec((D_MODEL, LANES), lambda i, j: (0, 0))],
        out_specs=[pl.BlockSpec((tm, tn), lambda i, j: (i, j)),
                   pl.BlockSpec((tm, LANES), lambda i, j: (i, 0))],
        out_shape=[jax.ShapeDtypeStruct((t, PROJ_COLS), BF16),
                   jax.ShapeDtypeStruct((t, LANES), F32)],
        scratch_shapes=[pltpu.VMEM((tm, D_MODEL), BF16)],
        compiler_params=_cparams(("arbitrary", "arbitrary")),
        name="inproj",
    )(x2, nw, w, wdt)


def _ssd_body(z_ref, xs_ref, b_ref, c_ref, dtr_ref, cw_ref, cb_ref, dtb_ref, alog_ref, dsk_ref, nw_ref,
              hexp_ref, shift_ref, y_ref, ubuf, xc, bc, cc, ybuf, s_ref, *, L):
    ci = pl.program_id(0)

    @pl.when(ci == 0)
    def _():
        ubuf[0:CONV_HALO, :] = jnp.zeros((CONV_HALO, SSD_CONV_DIM), BF16)
        s_ref[...] = jnp.zeros_like(s_ref)

    ubuf[CONV_HALO:CONV_HALO + L, 0:SSD_D_INNER] = xs_ref[...]
    ubuf[CONV_HALO:CONV_HALO + L, SSD_D_INNER:SSD_D_INNER + SSD_BC] = b_ref[...]
    ubuf[CONV_HALO:CONV_HALO + L, SSD_D_INNER + SSD_BC:SSD_CONV_DIM] = c_ref[...]
    cw = 512
    for cch in range(SSD_CONV_DIM // cw):
        cs_ = slice(cw * cch, cw * cch + cw)
        taps = jnp.dot(shift_ref[...], ubuf[:, cs_], preferred_element_type=F32)
        acc = cb_ref[:, cs_] + cw_ref[0:1, cs_] * taps[0:L]
        for k in range(1, SSD_CONV):
            acc = acc + cw_ref[k:k + 1, cs_] * taps[k * L:(k + 1) * L]
        act = _silu(acc)
        if cch < 4:
            xc[:, cs_] = act
        elif cch == 4:
            bc[...] = act
        else:
            cc[...] = act
    ubuf[0:CONV_HALO, :] = ubuf[L:L + CONV_HALO, :]

    dt_in = dtr_ref[...] + dtb_ref[...]
    dt = jnp.maximum(dt_in, 0.0) + jnp.log1p(jnp.exp(-jnp.abs(dt_in)))
    a = dt * (-jnp.exp(alog_ref[...]))
    row = lax.broadcasted_iota(I32, (L, L), 0)
    col = lax.broadcasted_iota(I32, (L, L), 1)
    causal = col <= row
    cs = jnp.dot(causal.astype(F32), a, precision=lax.Precision.HIGHEST, preferred_element_type=F32)
    cs_last = cs[L - 1:L, :]
    wmat = jnp.exp(cs_last - cs) * dt
    cs_t = cs.T
    dt_t = dt.T
    cdec = jnp.broadcast_to(jnp.exp(cs_last), (16, LANES))
    c1 = cdec.astype(BF16)
    r1 = cdec - c1.astype(F32)
    c2 = r1.astype(BF16)
    c3 = (r1 - c2.astype(F32)).astype(BF16)
    cdec_x = (jnp.dot(c1, hexp_ref[...], preferred_element_type=F32)
              + jnp.dot(c2, hexp_ref[...], preferred_element_type=F32)
              + jnp.dot(c3, hexp_ref[...], preferred_element_type=F32))[0:1, :]
    lane = lax.broadcasted_iota(I32, (L, LANES), 1)
    first = lane < SSD_HEAD_DIM

    for g in range(SSD_GROUPS):
        bg = bc[:, LANES * g:LANES * g + LANES]
        cg = cc[:, LANES * g:LANES * g + LANES]
        cb = lax.dot_general(cg.astype(BF16), bg.astype(BF16), (((1,), (1,)), ((), ())),
                             preferred_element_type=F32)
        bg_t = bg.T.astype(BF16)
        xw_parts = []
        for q in range(4):
            lanes_ = slice(512 * g + LANES * q, 512 * g + LANES * q + LANES)
            x_pair = xc[:, lanes_]
            s_pair = s_ref[g, :, LANES * q:LANES * q + LANES]
            rhs = jnp.concatenate([x_pair.astype(BF16), s_pair.astype(BF16)], axis=0)
            ys, wbs = [], []
            for e2 in range(2):
                h = 8 * g + 2 * q + e2
                cs_col = jnp.broadcast_to(cs[:, h:h + 1], (L, L))
                lm = jnp.exp(jnp.where(causal, cs_col - cs_t[h:h + 1, :], NEG))
                m = cb * lm * dt_t[h:h + 1, :]
                e_col = jnp.exp(jnp.broadcast_to(cs[:, h:h + 1], (L, LANES)))
                lhs = jnp.concatenate([m.astype(BF16), (cg * e_col).astype(BF16)], axis=1)
                ys.append(jnp.dot(lhs, rhs, preferred_element_type=F32))
                wbs.append(jnp.broadcast_to(wmat[:, h:h + 1], (L, LANES)))
            ybuf[:, lanes_] = jnp.where(first, ys[0], ys[1])
            xw_parts.append((x_pair * jnp.where(first, wbs[0], wbs[1])).astype(BF16))
        xw_g = jnp.concatenate(xw_parts, axis=1)
        s_ref[g] = (s_ref[g] * cdec_x[:, 512 * g:512 * g + 512]
                    + jnp.dot(bg_t, xw_g, preferred_element_type=F32))

    for g in range(SSD_GROUPS):
        gs = slice(512 * g, 512 * g + 512)
        zf = z_ref[:, gs].astype(F32)
        yg = (ybuf[:, gs] + dsk_ref[:, gs] * xc[:, gs]) * _silu(zf)
        ms = jnp.mean(yg * yg, axis=-1, keepdims=True)
        y_ref[:, gs] = (yg * lax.rsqrt(ms + EPS) * nw_ref[:, gs]).astype(BF16)


CONV_HALO = 16


def _conv_shift_matrix(L):
    s = np.zeros((SSD_CONV * L, CONV_HALO + L), np.float32)
    for k in range(SSD_CONV):
        s[k * L + np.arange(L), CONV_HALO + np.arange(L) - (SSD_CONV - 1) + k] = 1.0
    return jnp.asarray(s, BF16)


def _ssd(proj, dtraw, cw, cb, dtb, alog, dsk, nw, hexp):
    t = proj.shape[0]
    L = SSD_L
    shift = _conv_shift_matrix(L)
    full = lambda shape: pl.BlockSpec(shape, lambda i: (0,) * len(shape))
    return pl.pallas_call(
        functools.partial(_ssd_body, L=L),
        grid=(t // L,),
        in_specs=[pl.BlockSpec((L, SSD_D_INNER), lambda i: (i, COL_Z // SSD_D_INNER)),
                  pl.BlockSpec((L, SSD_D_INNER), lambda i: (i, COL_XS // SSD_D_INNER)),
                  pl.BlockSpec((L, SSD_BC), lambda i: (i, COL_B // SSD_BC)),
                  pl.BlockSpec((L, SSD_BC), lambda i: (i, COL_C // SSD_BC)),
                  pl.BlockSpec((L, LANES), lambda i: (i, 0)),
                  full((SSD_CONV, SSD_CONV_DIM)), full((1, SSD_CONV_DIM)),
                  full((1, LANES)), full((1, LANES)),
                  full((1, SSD_D_INNER)), full((1, SSD_D_INNER)), full((LANES, SSD_D_INNER)),
                  full((SSD_CONV * L, CONV_HALO + L))],
        out_specs=pl.BlockSpec((L, SSD_D_INNER), lambda i: (i, 0)),
        out_shape=jax.ShapeDtypeStruct((t, SSD_D_INNER), BF16),
        scratch_shapes=[pltpu.VMEM((CONV_HALO + L, SSD_CONV_DIM), BF16),
                        pltpu.VMEM((L, SSD_D_INNER), F32),
                        pltpu.VMEM((L, SSD_BC), F32),
                        pltpu.VMEM((L, SSD_BC), F32),
                        pltpu.VMEM((L, SSD_D_INNER), F32),
                        pltpu.VMEM((SSD_GROUPS, SSD_D_STATE, 512), F32)],
        compiler_params=_cparams(("arbitrary",)),
        name="ssd_scan",
    )(proj, proj, proj, proj, dtraw, cw, cb, dtb, alog, dsk, nw, hexp, shift)


def _attn_body(q_ref, k_ref, v_ref, cos_ref, sin_ref, qw_ref, kw_ref, hsum_ref, bias_ref, o_ref,
               q_s, k_s, v_s, o_b, m_b, l_b, *, blk):
    j = pl.program_id(1)
    cur = (j % 2) * blk
    lane = lax.broadcasted_iota(I32, (1, LANES), 1)
    c64 = lane % ATTN_HEAD_DIM
    low = c64 < (ROPE_DIM // 2)

    def norm_rope(raw, w_lane):
        xf = raw.astype(F32)
        x2 = xf * xf
        hi = x2.astype(BF16)
        lo = (x2 - hi.astype(F32)).astype(BF16)
        ss = (jnp.dot(hi, hsum_ref[...], preferred_element_type=F32)
              + jnp.dot(lo, hsum_ref[...], preferred_element_type=F32))
        y = xf * lax.rsqrt(ss * (1.0 / ATTN_HEAD_DIM) + EPS) * w_lane
        partner = jnp.where(low, pltpu.roll(y, LANES - ROPE_DIM // 2, 1), pltpu.roll(y, ROPE_DIM // 2, 1))
        return y * cos_ref[...] + partner * sin_ref[...]

    @pl.when(j == 0)
    def _():
        k_s[pl.ds(blk, blk), :] = jnp.zeros((blk, LANES), F32)
        v_s[pl.ds(blk, blk), :] = jnp.zeros((blk, LANES), F32)

    q_s[...] = norm_rope(q_ref[...], qw_ref[...]) * (ATTN_HEAD_DIM ** -0.5)
    k_s[pl.ds(cur, blk), :] = norm_rope(k_ref[...], kw_ref[...])
    v_s[pl.ds(cur, blk), :] = v_ref[...].astype(F32)
    lane_q = lax.broadcasted_iota(I32, (ATTN_SPAN, LANES), 1)
    first = lane_q < ATTN_HEAD_DIM
    mask0 = first.astype(F32)
    mask1 = 1.0 - mask0
    ones_v = jnp.ones((2 * ATTN_SPAN, LANES), BF16)

    def rows(ref, start, d):
        if d == 1:
            return ref[pl.ds(start, ATTN_SPAN), :]
        return ref[pl.ds(start, ATTN_SPAN, stride=d), :]

    def tile(br, start, d):
        qv = rows(q_s, start, d)
        qs = jnp.concatenate([qv * mask0, qv * mask1], axis=0).astype(BF16)
        prev_start = (cur + start - ATTN_SPAN * d) & (2 * blk - 1)
        own_start = cur + start
        kt = jnp.concatenate([rows(k_s, prev_start, d), rows(k_s, own_start, d)], axis=0).astype(BF16)
        vt = jnp.concatenate([rows(v_s, prev_start, d), rows(v_s, own_start, d)], axis=0).astype(BF16)
        s = lax.dot_general(qs, kt, (((1,), (1,)), ((), ())), preferred_element_type=F32)
        no_prev = jnp.logical_and(j == 0, start < ATTN_SPAN * d)
        s = s + bias_ref[no_prev.astype(I32)]
        m2 = jnp.max(s, axis=-1, keepdims=True)
        p = jnp.exp(s - m2).astype(BF16)
        o2 = jnp.dot(p, jnp.concatenate([vt, ones_v], axis=1), preferred_element_type=F32)
        m_rep = jnp.broadcast_to(m2, (2 * ATTN_SPAN, LANES))
        if d == 1:
            sl = pl.ds(start, ATTN_SPAN)
        else:
            sl = pl.ds(start, ATTN_SPAN, stride=d)
        o_b[br, sl, :] = jnp.where(first, o2[:ATTN_SPAN, :LANES], o2[ATTN_SPAN:, :LANES])
        l_b[br, sl, :] = jnp.where(first, o2[:ATTN_SPAN, LANES:], o2[ATTN_SPAN:, LANES:])
        m_b[br, sl, :] = jnp.where(first, m_rep[:ATTN_SPAN], m_rep[ATTN_SPAN:])

    for br, d in enumerate(DILATIONS):
        n_res = d
        n_sub = blk // (d * ATTN_SPAN)

        def loop_body(i, carry, br=br, d=d, n_res=n_res):
            r = i % n_res
            sub = i // n_res
            tile(br, r + sub * (d * ATTN_SPAN), d)
            return carry

        lax.fori_loop(0, n_res * n_sub, loop_body, 0, unroll=8)

    def merge(c, carry):
        rs = pl.ds(pl.multiple_of(c * ATTN_SPAN, ATTN_SPAN), ATTN_SPAN)
        ms = [m_b[br, rs, :] for br in range(len(DILATIONS))]
        m_all = jnp.maximum(jnp.maximum(ms[0], ms[1]), ms[2])
        ws = [jnp.exp(m - m_all) for m in ms]
        num = ws[0] * o_b[0, rs, :] + ws[1] * o_b[1, rs, :] + ws[2] * o_b[2, rs, :]
        den = ws[0] * l_b[0, rs, :] + ws[1] * l_b[1, rs, :] + ws[2] * l_b[2, rs, :]
        o_ref[rs, :] = (num / den).astype(BF16)
        return carry

    lax.fori_loop(0, blk // ATTN_SPAN, merge, 0, unroll=8)


def _attention(proj, cos_t, sin_t, qw_lane, kw_lane, hsum, bias):
    t = proj.shape[0]
    blk = ATT_BLK
    n_hp = ATTN_HEADS // 2
    colblk = lambda base: (lambda hp, j: (j, base // LANES + hp))
    full = lambda shape: pl.BlockSpec(shape, lambda hp, j: (0,) * len(shape))
    return pl.pallas_call(
        functools.partial(_attn_body, blk=blk),
        grid=(n_hp, t // blk),
        in_specs=[pl.BlockSpec((blk, LANES), colblk(COL_Q)),
                  pl.BlockSpec((blk, LANES), colblk(COL_K)),
                  pl.BlockSpec((blk, LANES), colblk(COL_V)),
                  pl.BlockSpec((blk, LANES), lambda hp, j: (j, 0)),
                  pl.BlockSpec((blk, LANES), lambda hp, j: (j, 0)),
                  full((1, LANES)), full((1, LANES)), full((LANES, LANES)),
                  full((2, 2 * ATTN_SPAN, 2 * ATTN_SPAN))],
        out_specs=pl.BlockSpec((blk, LANES), lambda hp, j: (j, hp)),
        out_shape=jax.ShapeDtypeStruct((t, ATTN_D), BF16),
        scratch_shapes=[pltpu.VMEM((blk, LANES), F32),
                        pltpu.VMEM((2 * blk, LANES), F32),
                        pltpu.VMEM((2 * blk, LANES), F32),
                        pltpu.VMEM((len(DILATIONS), blk, LANES), F32),
                        pltpu.VMEM((len(DILATIONS), blk, LANES), F32),
                        pltpu.VMEM((len(DILATIONS), blk, LANES), F32)],
        compiler_params=_cparams(("arbitrary", "arbitrary")),
        name="dilated_attn",
    )(proj, proj, proj, cos_t, sin_t, qw_lane, kw_lane, hsum, bias)


def _pack_bf16_pair(lo_f32, hi_f32):
    lo_bits = pltpu.bitcast(lo_f32.astype(BF16).astype(F32), U32)
    hi_bits = pltpu.bitcast(hi_f32.astype(BF16).astype(F32), U32)
    return (lo_bits >> 16) | (hi_bits & jnp.uint32(0xFFFF0000))


def _unpack_bf16_pair(w):
    lo = pltpu.bitcast(w << 16, F32)
    hi = pltpu.bitcast(w & jnp.uint32(0xFFFF0000), F32)
    return lo, hi


def _outproj_body(ys_ref, ya_ref, x_ref, w_ref, nw_ref, wr_ref, h_ref, hn_ref, lg_ref):
    acc = jnp.dot(ys_ref[...], w_ref[0:SSD_D_INNER, :], preferred_element_type=F32)
    acc = acc + jnp.dot(ya_ref[...], w_ref[SSD_D_INNER:SSD_D_INNER + ATTN_D, :], preferred_element_type=F32)
    h = x_ref[...] + acc
    h_ref[...] = h
    ms = jnp.mean(h * h, axis=-1, keepdims=True)
    hn = h * lax.rsqrt(ms + EPS) * nw_ref[...]
    half = D_MODEL // 2
    hn_ref[...] = _pack_bf16_pair(hn[:, :half], hn[:, half:])
    hn_hi = hn.astype(BF16)
    hn_lo = (hn - hn_hi.astype(F32)).astype(BF16)
    nt_dims = (((1,), (1,)), ((), ()))
    lg_ref[...] = (lax.dot_general(wr_ref[0], hn_hi, nt_dims, preferred_element_type=F32)
                   + lax.dot_general(wr_ref[0], hn_lo, nt_dims, preferred_element_type=F32)
                   + lax.dot_general(wr_ref[1], hn_hi, nt_dims, preferred_element_type=F32))


def _outproj(yssd, yattn, x2, w, nw, wr_t):
    t = x2.shape[0]
    tm = 512
    full = lambda shape: pl.BlockSpec(shape, lambda i: (0,) * len(shape))
    return pl.pallas_call(
        _outproj_body,
        grid=(t // tm,),
        in_specs=[pl.BlockSpec((tm, SSD_D_INNER), lambda i: (i, 0)),
                  pl.BlockSpec((tm, ATTN_D), lambda i: (i, 0)),
                  pl.BlockSpec((tm, D_MODEL), lambda i: (i, 0)),
                  full((SSD_D_INNER + ATTN_D, D_MODEL)), full((1, D_MODEL)), full((2, LANES, D_MODEL))],
        out_specs=[pl.BlockSpec((tm, D_MODEL), lambda i: (i, 0)),
                   pl.BlockSpec((tm, D_MODEL // 2), lambda i: (i, 0)),
                   pl.BlockSpec((LANES, tm), lambda i: (0, i))],
        out_shape=[jax.ShapeDtypeStruct((t, D_MODEL), F32),
                   jax.ShapeDtypeStruct((t, D_MODEL // 2), U32),
                   jax.ShapeDtypeStruct((LANES, t), F32)],
        compiler_params=_cparams(("arbitrary",)),
        name="outproj",
    )(yssd, yattn, x2, w, nw, wr_t)


def _route_body(lg_ref, bias_ref, u_ref, oi_ref, of_ref, cnt_ref, carry, *, tt):
    i = pl.program_id(0)

    @pl.when(i == 0)
    def _():
        carry[...] = jnp.zeros_like(carry)

    lg = lg_ref[...] + bias_ref[...]
    sub8 = lax.broadcasted_iota(I32, (8, tt), 0)
    g = lg[0:8]
    gmax = jnp.max(g, axis=0, keepdims=True)
    gidx = jnp.min(jnp.where(g == gmax, sub8, 8), axis=0, keepdims=True)
    gval = 1.0 / jnp.sum(jnp.exp(g - gmax), axis=0, keepdims=True)
    esel = jnp.zeros((8, tt), F32)
    for grp in range(N_GROUPS):
        esel = jnp.where(gidx == grp, lg[8 + 8 * grp:16 + 8 * grp], esel)
    m1 = jnp.max(esel, axis=0, keepdims=True)
    i1 = jnp.min(jnp.where(esel == m1, sub8, 8), axis=0, keepdims=True)
    em = jnp.where(sub8 == i1, -jnp.inf, esel)
    m2 = jnp.max(em, axis=0, keepdims=True)
    i2 = jnp.min(jnp.where(em == m2, sub8, 8), axis=0, keepdims=True)
    r = jnp.exp(m2 - m1)
    w1 = gval * (1.0 / (1.0 + r))
    w2 = gval * (r / (1.0 + r))
    e1 = gidx * EXPERTS_PER_GROUP + i1
    e2 = gidx * EXPERTS_PER_GROUP + i2
    sub64 = lax.broadcasted_iota(I32, (N_EXPERTS, tt), 0)
    oh1 = sub64 == e1
    oh2 = sub64 == e2
    oh = oh1.astype(F32) + oh2.astype(F32)
    pref = jnp.dot(oh.astype(BF16), u_ref[...], preferred_element_type=F32)
    excl = carry[...] + pref - 1.0
    rank1 = jnp.sum(jnp.where(oh1, excl, 0.0), axis=0, keepdims=True)
    rank2 = jnp.sum(jnp.where(oh2, excl, 0.0), axis=0, keepdims=True)
    carry[...] = carry[...] + pref[:, tt - 1:tt]
    zi = jnp.zeros((4, tt), I32)
    oi_ref[...] = jnp.concatenate([e1, e2, rank1.astype(I32), rank2.astype(I32), zi], axis=0)
    of_ref[...] = jnp.concatenate([w1, w2, jnp.zeros((6, tt), F32)], axis=0)
    cnt_ref[...] = jnp.broadcast_to(carry[...], (N_EXPERTS, LANES))


def _route(lg_t, bias_col, utri):
    t = lg_t.shape[1]
    tt = utri.shape[0]
    return pl.pallas_call(
        functools.partial(_route_body, tt=tt),
        grid=(t // tt,),
        in_specs=[pl.BlockSpec((LANES, tt), lambda i: (0, i)),
                  pl.BlockSpec((LANES, 1), lambda i: (0, 0)),
                  pl.BlockSpec((tt, tt), lambda i: (0, 0))],
        out_specs=[pl.BlockSpec((8, tt), lambda i: (0, i)),
                   pl.BlockSpec((8, tt), lambda i: (0, i)),
                   pl.BlockSpec((N_EXPERTS, LANES), lambda i: (0, 0))],
        out_shape=[jax.ShapeDtypeStruct((8, t), I32),
                   jax.ShapeDtypeStruct((8, t), F32),
                   jax.ShapeDtypeStruct((N_EXPERTS, LANES), F32)],
        scratch_shapes=[pltpu.VMEM((N_EXPERTS, 1), F32)],
        compiler_params=_cparams(("arbitrary",)),
        name="route",
    )(lg_t, bias_col, utri)


def _dest_body(oi_ref, poff_ref, d_ref, *, tt):
    sub64 = lax.broadcasted_iota(I32, (N_EXPERTS, tt), 0)
    oi = oi_ref[...]
    poff = poff_ref[...]
    d1 = jnp.sum(jnp.where(sub64 == oi[0:1], poff, 0), axis=0, keepdims=True) + oi[2:3]
    d2 = jnp.sum(jnp.where(sub64 == oi[1:2], poff, 0), axis=0, keepdims=True) + oi[3:4]
    d_ref[...] = jnp.concatenate([d1, d2, jnp.zeros((6, tt), I32)], axis=0)


def _dest(oi, poff_col):
    t = oi.shape[1]
    tt = min(t, 2048)
    return pl.pallas_call(
        functools.partial(_dest_body, tt=tt),
        grid=(t // tt,),
        in_specs=[pl.BlockSpec((8, tt), lambda i: (0, i)),
                  pl.BlockSpec((N_EXPERTS, 1), lambda i: (0, 0))],
        out_specs=pl.BlockSpec((8, tt), lambda i: (0, i)),
        out_shape=jax.ShapeDtypeStruct((8, t), I32),
        compiler_params=_cparams(("arbitrary",)),
        name="dest_rows",
    )(oi, poff_col)


def _rowtok_body(d1_ref, d2_ref, rt_ref, *, t, r_alloc):
    def clear(i, c):
        rt_ref[i] = 0
        return c

    lax.fori_loop(0, r_alloc, clear, 0, unroll=128)

    def put(i, c):
        rt_ref[d1_ref[i]] = i
        rt_ref[d2_ref[i]] = i
        return c

    lax.fori_loop(0, t, put, 0, unroll=128)


def _rowtok(d1, d2, r_alloc):
    t = d1.shape[0]
    smem = pl.BlockSpec(memory_space=pltpu.SMEM)
    return pl.pallas_call(
        functools.partial(_rowtok_body, t=t, r_alloc=r_alloc),
        in_specs=[smem, smem],
        out_specs=smem,
        out_shape=jax.ShapeDtypeStruct((r_alloc,), I32),
        name="row_tokens",
    )(d1, d2)


def _experts_body(rt_ref, ie_ref, ir_ref, int_ref, ifl_ref, ni_ref, bi_ref, x_hbm,
                  wga_ref, wgb_ref, wua_ref, wub_ref, wdla_ref, wdlb_ref, wdha_ref, wdhb_ref,
                  y_hbm, xlo, xhi, hid, ybuf, ring, gsem, ysem, *, nsub):
    w = pl.program_id(0)
    s = pl.program_id(1)
    nt = int_ref[w]
    nfill = ifl_ref[w]
    row0 = ir_ref[w]
    tile0 = row0 // MOE_TM
    n_items = ni_ref[0]
    half = D_MODEL // 2
    hn = D_EXPERT // nsub

    def gather_quarter(tile, q):
        slot = tile % RING_TILES
        base = tile * MOE_TM + q * GATHER_GROUP
        for j in range(GATHER_GROUP):
            tok = rt_ref[base + j]
            pltpu.make_async_copy(x_hbm.at[pl.ds(tok, 1)], ring.at[slot, pl.ds(q * GATHER_GROUP + j, 1)],
                                  gsem.at[slot]).start()

    def ring_wait(tile):
        slot = tile % RING_TILES
        pltpu.make_async_copy(x_hbm.at[pl.ds(0, MOE_TM)], ring.at[slot], gsem.at[slot]).wait()

    def tile_copy_out(r0, tl):
        rs = pl.ds(pl.multiple_of(tl * MOE_TM, MOE_TM), MOE_TM)
        dst = pl.ds(pl.multiple_of(r0 + tl * MOE_TM, MOE_TM), MOE_TM)
        return pltpu.make_async_copy(ybuf.at[rs], y_hbm.at[dst], ysem)

    def for_tiles(n, fn):
        def body(tl, c):
            fn(tl)
            return c

        lax.fori_loop(0, n, body, 0)

    @pl.when(jnp.logical_and(w == 0, s == 0))
    def _():
        for q in range(GROUPS_PER_TILE):
            for_tiles(ITEM_TILES, lambda tl, q=q: gather_quarter(tl, q))

    @pl.when(jnp.logical_and(nt > 0, s == 0))
    def _():
        for_tiles(nt, lambda tl: ring_wait(tile0 + tl))

    def gateup(tl, step):
        rs = pl.ds(pl.multiple_of(tl * MOE_TM, MOE_TM), MOE_TM)
        if step == 0:
            lo, hi = _unpack_bf16_pair(ring[(tile0 + tl) % RING_TILES])
            a = lo.astype(BF16)
            b = hi.astype(BF16)
            xlo[rs, :] = a
            xhi[rs, :] = b
        else:
            a = xlo[rs, :]
            b = xhi[rs, :]
        gate = (jnp.dot(a, wga_ref[...].astype(BF16), preferred_element_type=F32)
                + jnp.dot(b, wgb_ref[...].astype(BF16), preferred_element_type=F32))
        up = (jnp.dot(a, wua_ref[...].astype(BF16), preferred_element_type=F32)
              + jnp.dot(b, wub_ref[...].astype(BF16), preferred_element_type=F32))
        hid[rs, step * hn:(step + 1) * hn] = (_silu(gate) * up).astype(BF16)
        gather_quarter(tile0 + ITEM_TILES + tl, step)

    def down(tl, step):
        rs = pl.ds(pl.multiple_of(tl * MOE_TM, MOE_TM), MOE_TM)
        gather_quarter(tile0 + ITEM_TILES + tl, nsub + step)
        ha = hid[rs, 0:D_EXPERT // 2]
        hb = hid[rs, D_EXPERT // 2:D_EXPERT]
        ybuf[rs, step * hn:(step + 1) * hn] = _pack_bf16_pair(
            jnp.dot(ha, wdla_ref[...].astype(BF16), preferred_element_type=F32)
            + jnp.dot(hb, wdlb_ref[...].astype(BF16), preferred_element_type=F32),
            jnp.dot(ha, wdha_ref[...].astype(BF16), preferred_element_type=F32)
            + jnp.dot(hb, wdhb_ref[...].astype(BF16), preferred_element_type=F32))

    for step in range(nsub):
        @pl.when(jnp.logical_and(nt > 0, s == step))
        def _(step=step):
            for_tiles(nt, lambda tl: gateup(tl, step))

    @pl.when(jnp.logical_and(jnp.logical_and(nt > 0, w > 0), s == nsub))
    def _():
        for_tiles(int_ref[jnp.maximum(w - 1, 0)], lambda tl: tile_copy_out(row0, tl).wait())

    for step in range(nsub):
        @pl.when(jnp.logical_and(nt > 0, s == nsub + step))
        def _(step=step):
            for_tiles(nt, lambda tl: down(tl, step))

    @pl.when(jnp.logical_and(nt > 0, s == 2 * nsub - 1))
    def _():
        for_tiles(nt, lambda tl: tile_copy_out(row0, tl).start())

    @pl.when(jnp.logical_and(w == n_items - 1, s == 2 * nsub - 1))
    def _():
        for_tiles(nt, lambda tl: tile_copy_out(row0, tl).wait())
        for_tiles(ITEM_TILES, lambda tl: ring_wait(tile0 + nt + tl))

    @pl.when(jnp.logical_and(nfill > 0, s == 0))
    def _():
        ybuf[...] = jnp.zeros_like(ybuf)
        for_tiles(nfill, lambda tl: tile_copy_out(row0, tl).start())
        for_tiles(nfill, lambda tl: tile_copy_out(row0, tl).wait())


ITEM_TILES = 4
RING_TILES = 2 * ITEM_TILES
GATHER_GROUP = 64
GROUPS_PER_TILE = MOE_TM // GATHER_GROUP


def _experts(row_tok, item_e, item_row0, item_nt, item_fill, n_items, blk_idx, hn2p, w_gate, w_up, w_down, r_alloc):
    ni = item_e.shape[0]
    nsub = 2
    assert GROUPS_PER_TILE == 2 * nsub
    hn = D_EXPERT // nsub
    rows = ITEM_TILES * MOE_TM
    nstep = 2 * nsub

    ns = ni * nstep


    def gu_map(kh):
        return lambda w, s, rt, ie, ir, nt, fl, n, bi: (bi[w * nstep + s], kh, bi[ns + w * nstep + s])

    def dl_map(kh):
        return lambda w, s, rt, ie, ir, nt, fl, n, bi: (bi[2 * ns + w * nstep + s], kh, bi[3 * ns + w * nstep + s])

    def dh_map(kh):
        return lambda w, s, rt, ie, ir, nt, fl, n, bi: (bi[2 * ns + w * nstep + s], kh,
                                                        nsub + bi[3 * ns + w * nstep + s])

    return pl.pallas_call(
        functools.partial(_experts_body, nsub=nsub),
        grid_spec=pltpu.PrefetchScalarGridSpec(
            num_scalar_prefetch=7,
            grid=(ni, nstep),
            in_specs=[pl.BlockSpec(memory_space=pl.ANY),
                      pl.BlockSpec((None, D_MODEL // 2, hn), gu_map(0)),
                      pl.BlockSpec((None, D_MODEL // 2, hn), gu_map(1)),
                      pl.BlockSpec((None, D_MODEL // 2, hn), gu_map(0)),
                      pl.BlockSpec((None, D_MODEL // 2, hn), gu_map(1)),
                      pl.BlockSpec((None, D_EXPERT // 2, hn), dl_map(0)),
                      pl.BlockSpec((None, D_EXPERT // 2, hn), dl_map(1)),
                      pl.BlockSpec((None, D_EXPERT // 2, hn), dh_map(0)),
                      pl.BlockSpec((None, D_EXPERT // 2, hn), dh_map(1))],
            out_specs=pl.BlockSpec(memory_space=pl.ANY),
            scratch_shapes=[pltpu.VMEM((rows, D_MODEL // 2), BF16),
                            pltpu.VMEM((rows, D_MODEL // 2), BF16),
                            pltpu.VMEM((rows, D_EXPERT), BF16),
                            pltpu.VMEM((rows, D_MODEL // 2), U32),
                            pltpu.VMEM((RING_TILES, MOE_TM, D_MODEL // 2), U32),
                            pltpu.SemaphoreType.DMA((RING_TILES,)),
                            pltpu.SemaphoreType.DMA(())]),
        out_shape=jax.ShapeDtypeStruct((r_alloc, D_MODEL // 2), U32),
        compiler_params=_cparams(("arbitrary", "arbitrary")),
        name="experts",
    )(row_tok, item_e, item_row0, item_nt, item_fill, n_items, blk_idx, hn2p,
      w_gate, w_gate, w_up, w_up, w_down, w_down, w_down, w_down)


def _combine_body(d1_ref, d2_ref, h_ref, w_ref, y_ref, o_ref, buf, sem, *, tt, n_steps):
    i = pl.program_id(0)

    sub = 8

    def issue(step, slot):
        def body(r8, c):
            for j in range(sub):
                t0 = step * tt + r8 * sub + j
                pltpu.make_async_copy(y_ref.at[pl.ds(d1_ref[t0], 1)], buf.at[slot, 0, r8, pl.ds(j, 1)],
                                      sem.at[slot]).start()
                pltpu.make_async_copy(y_ref.at[pl.ds(d2_ref[t0], 1)], buf.at[slot, 1, r8, pl.ds(j, 1)],
                                      sem.at[slot]).start()
            return c

        lax.fori_loop(0, tt // sub, body, 0, unroll=4)

    @pl.when(i == 0)
    def _():
        issue(0, 0)

    @pl.when(i + 1 < n_steps)
    def _():
        issue(i + 1, (i + 1) % 2)

    slot = i % 2
    for k in range(2):
        pltpu.make_async_copy(buf.at[1 - slot, k], buf.at[slot, k], sem.at[slot]).wait()
    a_lo, a_hi = _unpack_bf16_pair(buf[slot, 0].reshape(tt, D_MODEL // 2))
    b_lo, b_hi = _unpack_bf16_pair(buf[slot, 1].reshape(tt, D_MODEL // 2))
    w1 = w_ref[:, 0:1]
    w2 = w_ref[:, 1:2]
    half = D_MODEL // 2
    o_ref[:, 0:half] = h_ref[:, 0:half] + (a_lo * w1 + b_lo * w2)
    o_ref[:, half:D_MODEL] = h_ref[:, half:D_MODEL] + (a_hi * w1 + b_hi * w2)


def _combine(d1, d2, h1, w_tok, y_rows):
    t = h1.shape[0]
    tt = 256
    n_steps = t // tt
    return pl.pallas_call(
        functools.partial(_combine_body, tt=tt, n_steps=n_steps),
        grid_spec=pltpu.PrefetchScalarGridSpec(
            num_scalar_prefetch=2,
            grid=(n_steps,),
            in_specs=[pl.BlockSpec((tt, D_MODEL), lambda i, d1, d2: (i, 0)),
                      pl.BlockSpec((tt, LANES), lambda i, d1, d2: (i, 0)),
                      pl.BlockSpec(memory_space=pl.ANY)],
            out_specs=pl.BlockSpec((tt, D_MODEL), lambda i, d1, d2: (i, 0)),
            scratch_shapes=[pltpu.VMEM((2, 2, tt // 8, 8, D_MODEL // 2), U32),
                            pltpu.SemaphoreType.DMA((2,))]),
        out_shape=jax.ShapeDtypeStruct((t, D_MODEL), F32),
        compiler_params=_cparams(("arbitrary",)),
        name="moe_combine",
    )(d1, d2, h1, w_tok, y_rows)


def _band_bias():
    qi = np.arange(ATTN_SPAN)[:, None]
    kj = np.arange(ATTN_SPAN)[None, :]
    prev = np.where(kj >= qi, 0.0, NEG)
    own = np.where(kj <= qi, 0.0, NEG)
    with_prev = np.concatenate([prev, own], axis=1)
    no_prev = np.concatenate([np.full_like(prev, NEG), own], axis=1)
    both = np.stack([np.tile(with_prev, (2, 1)), np.tile(no_prev, (2, 1))])
    return jnp.asarray(both, F32)


def kernel(x, positions, norm1_w, w_in, conv_w, conv_b, dt_bias, A_log, D_skip, ssd_norm_w, q_norm_w, k_norm_w,
           w_out, norm2_w, w_group_router, b_group_router, w_expert_router, b_expert_router, w_gate, w_up, w_down):
    b, s, _ = x.shape
    assert b == 1 and norm1_w.shape[0] == 1
    t = s
    x2 = x.reshape(t, D_MODEL)

    w_in0 = w_in[0]
    zc, xc_, bc_, cc_, dtc, qc, kc, vc = np.cumsum((0, 2048, 2048, 512, 512, 32, 1024, 1024))
    assert dtc == COL_Q and qc - dtc == SSD_HEADS
    w_main = jnp.concatenate([w_in0[:, :dtc].astype(BF16), w_in0[:, qc:].astype(BF16)], axis=1)
    w_dt = jnp.pad(w_in0[:, dtc:qc], ((0, 0), (0, LANES - SSD_HEADS))).astype(BF16)
    pad_h = lambda v: jnp.pad(v.astype(F32), (0, LANES - SSD_HEADS)).reshape(1, LANES)
    lane = np.arange(LANES)
    c64 = lane % ATTN_HEAD_DIM
    inv_freq = 1.0 / (ROPE_THETA ** (jnp.arange(0, ROPE_DIM, 2, dtype=F32) / ROPE_DIM))
    f_lane = jnp.where(jnp.asarray(c64 < ROPE_DIM), inv_freq[jnp.asarray(c64 % (ROPE_DIM // 2))], 0.0).reshape(1, LANES)
    sg_lane = jnp.asarray(np.where(c64 < ROPE_DIM // 2, -1.0, np.where(c64 < ROPE_DIM, 1.0, 0.0)), F32).reshape(1, LANES)
    hsum = jnp.asarray((lane[:, None] // ATTN_HEAD_DIM) == (lane[None, :] // ATTN_HEAD_DIM), BF16)
    hexp = jnp.asarray(np.arange(LANES)[:, None] == (np.arange(SSD_D_INNER)[None, :] // SSD_HEAD_DIM), BF16)
    qw_lane = jnp.tile(q_norm_w[0].astype(F32), 2).reshape(1, LANES)
    kw_lane = jnp.tile(k_norm_w[0].astype(F32), 2).reshape(1, LANES)
    dsk = jnp.repeat(D_skip[0].astype(F32), SSD_HEAD_DIM).reshape(1, SSD_D_INNER)
    wr = jnp.concatenate([w_group_router[0],
                          jnp.transpose(w_expert_router[0], (1, 0, 2)).reshape(D_MODEL, N_EXPERTS)], axis=1)
    wr_f = jnp.pad(wr.T.astype(F32), ((0, LANES - N_GROUPS - N_EXPERTS), (0, 0)))
    wr_hi = wr_f.astype(BF16)
    wr_t = jnp.stack([wr_hi, (wr_f - wr_hi.astype(F32)).astype(BF16)])
    br = jnp.pad(jnp.concatenate([b_group_router[0], b_expert_router[0].reshape(-1)]).astype(F32),
                 (0, LANES - N_GROUPS - N_EXPERTS)).reshape(LANES, 1)

    cos_t, sin_t = _rope_table(positions.reshape(t, 1), f_lane, sg_lane)
    proj, dtraw = _inproj(x2, norm1_w.astype(F32), w_main, w_dt)
    y_ssd = _ssd(proj, dtraw, conv_w[0].astype(F32), conv_b.astype(F32), pad_h(dt_bias[0]), pad_h(A_log[0]),
                 dsk, ssd_norm_w.astype(F32), hexp)
    y_attn = _attention(proj, cos_t, sin_t, qw_lane, kw_lane, hsum, _band_bias())
    h1, hn2p, lg_t = _outproj(y_ssd, y_attn, x2, w_out[0].astype(BF16), norm2_w.astype(F32), wr_t)

    rt_tt = 1024
    utri = jnp.asarray(np.arange(rt_tt)[:, None] <= np.arange(rt_tt)[None, :], BF16)
    oi, of, cnt = _route(lg_t, br, utri)
    counts = cnt[:, 0].astype(I32)
    n_tiles_max = (2 * t + N_EXPERTS * (MOE_TM - 1) + MOE_TM - 1) // MOE_TM
    r_alloc = n_tiles_max * MOE_TM
    tiles_e = (counts + MOE_TM - 1) // MOE_TM
    tile_end = jnp.cumsum(tiles_e)
    n_used = tile_end[-1]
    poff = ((tile_end - tiles_e) * MOE_TM).reshape(N_EXPERTS, 1)
    nt = n_used.reshape(1).astype(I32)
    dst = _dest(oi, poff.astype(I32))
    row_tok = _rowtok(dst[0], dst[1], r_alloc + ITEM_TILES * MOE_TM)

    items_e = (tiles_e + ITEM_TILES - 1) // ITEM_TILES
    item_end = jnp.cumsum(items_e)
    n_items = item_end[-1]
    ni_max = (n_tiles_max + (ITEM_TILES - 1) * N_EXPERTS + ITEM_TILES - 1) // ITEM_TILES
    w_idx = jnp.arange(ni_max, dtype=I32)
    w_live = w_idx < n_items
    w_c = jnp.minimum(w_idx, n_items - 1)
    item_e = jnp.searchsorted(item_end, w_c, side="right").astype(I32)
    k_in_e = w_c - (item_end - items_e)[item_e]
    spare = w_idx - n_items
    item_row0 = jnp.where(w_live, poff[item_e, 0] + k_in_e * (ITEM_TILES * MOE_TM),
                          (n_used + ITEM_TILES * spare) * MOE_TM).astype(I32)
    item_nt = jnp.where(w_live, jnp.clip(tiles_e[item_e] - ITEM_TILES * k_in_e, 0, ITEM_TILES), 0).astype(I32)
    item_fill = jnp.where(w_live, 0,
                          jnp.clip(n_tiles_max - n_used - ITEM_TILES * spare, 0, ITEM_TILES)).astype(I32)

    e_last = item_e[jnp.maximum(n_items - 1, 0)]
    e_next = jnp.where(w_idx + 1 < n_items, item_e[jnp.minimum(w_idx + 1, ni_max - 1)], e_last)
    e_prev = item_e[jnp.maximum(w_idx - 1, 0)]
    gu_e = jnp.stack([item_e, item_e, e_next, e_next], axis=1)
    gu_h = jnp.broadcast_to(jnp.asarray([0, 1, 0, 0], I32), (ni_max, 4))
    dn_e = jnp.stack([e_prev, e_prev, item_e, item_e], axis=1)
    dn_h = jnp.where((w_idx == 0)[:, None], jnp.asarray([0, 0, 0, 1], I32), jnp.asarray([1, 1, 0, 1], I32))
    dead = jnp.logical_not(w_live)[:, None]
    gu_e = jnp.where(dead, e_last, gu_e)
    gu_h = jnp.where(dead, 0, gu_h)
    dn_e = jnp.where(dead, e_last, dn_e)
    dn_h = jnp.where(dead, 1, dn_h)
    blk_idx = jnp.concatenate([gu_e.reshape(-1), gu_h.reshape(-1), dn_e.reshape(-1), dn_h.reshape(-1)]).astype(I32)

    y_rows = _experts(row_tok, item_e, item_row0, item_nt, item_fill, n_items.reshape(1).astype(I32), blk_idx,
                      hn2p, w_gate[0], w_up[0], w_down[0], r_alloc)
    w_tok = jnp.pad(of[0:2].T, ((0, 0), (0, LANES - 2)))
    out = _combine(dst[0], dst[1], h1, w_tok, y_rows)
    return out.reshape(b, s, D_MODEL)
```
